```python
import jax, jax.numpy as jnp
from jax import lax
import numpy as np

D_MODEL = 2048
BATCH = 1
SEQ = 16384
DEPTH = 1
DEC_BATCH = 8
DEC_SEQ = 4096
PAST_LEN = 128

N_HEADS = 16
QK_NOPE = 128
QK_ROPE = 64
V_HEAD = 128
Q_LORA = 768
KV_LORA = 512
ROPE_THETA = 10000.0
Q_BLOCK = 128
D_CONV = 1024
CONV_GROUPS = 8
CONV_W = 3
D_FF = 5632
N_BRANCH = 2
EPS = 1e-6

OFF_CQ = 0
OFF_CKV = OFF_CQ + Q_LORA
OFF_KR = OFF_CKV + KV_LORA
OFF_CB = OFF_KR + QK_ROPE
OFF_CC = OFF_CB + D_CONV
OFF_CX = OFF_CC + D_CONV
OFF_G = OFF_CX + D_CONV
D_IN = OFF_G + N_BRANCH * D_MODEL

kernel_name = "hybrid_mla_shortconv_macaron_encoder"


def _rms(x, g):
    x32 = x.astype(jnp.float32)
    y = x32 * lax.rsqrt(jnp.mean(x32 * x32, axis=-1, keepdims=True) + EPS)
    return (y * g.astype(jnp.float32)).astype(x.dtype)


def _swiglu(x, w_gu, w_down):
    gate, up = jnp.split(x @ w_gu, 2, axis=-1)
    return (jax.nn.silu(gate) * up) @ w_down


def _rope(x, cos, sin):
    x1, x2 = jnp.split(x, 2, axis=-1)
    cos = cos.astype(x.dtype)
    sin = sin.astype(x.dtype)
    return jnp.concatenate([x1 * cos - x2 * sin, x2 * cos + x1 * sin], axis=-1)


def _short_conv(z, w):
    zp = jnp.pad(z, ((0, 0), (1, 1), (0, 0)))
    return zp[:, :-2] * w[0] + zp[:, 1:-1] * w[1] + zp[:, 2:] * w[2]


def _mla(c_q, c_kv, k_rope, q_norm, kv_norm, w_uq, w_ukv):
    B, S, _ = c_q.shape
    q = (_rms(c_q, q_norm) @ w_uq).reshape(B, S, N_HEADS, QK_NOPE + QK_ROPE)
    q_nope, q_rope = q[..., :QK_NOPE], q[..., QK_NOPE:]
    kv = (_rms(c_kv, kv_norm) @ w_ukv).reshape(B, S, N_HEADS, QK_NOPE + V_HEAD)
    k_nope, v = kv[..., :QK_NOPE], kv[..., QK_NOPE:]
    pos = jnp.arange(S, dtype=jnp.float32)
    inv_freq = ROPE_THETA ** (-jnp.arange(0, QK_ROPE, 2, dtype=jnp.float32) / QK_ROPE)
    ang = pos[:, None] * inv_freq[None, :]
    cos, sin = jnp.cos(ang), jnp.sin(ang)
    scale = (QK_NOPE + QK_ROPE) ** -0.5
    q_rope = _rope(q_rope, cos[:, None, :], sin[:, None, :]) * scale
    q_nope = q_nope * scale
    k_rope = _rope(k_rope, cos, sin)
    nb = S // Q_BLOCK
    qn_b = q_nope.reshape(B, nb, Q_BLOCK, N_HEADS, QK_NOPE).swapaxes(0, 1)
    qr_b = q_rope.reshape(B, nb, Q_BLOCK, N_HEADS, QK_ROPE).swapaxes(0, 1)

    def block(args):
        qn, qr = args
        s = (jnp.einsum('bqhd,bkhd->bhqk', qn, k_nope, preferred_element_type=jnp.float32)
             + jnp.einsum('bqhr,bkr->bhqk', qr, k_rope, preferred_element_type=jnp.float32))
        p = jax.nn.softmax(s, axis=-1).astype(v.dtype)
        return jnp.einsum('bhqk,bkhd->bqhd', p, v)

    o = lax.map(block, (qn_b, qr_b))
    return o.swapaxes(0, 1).reshape(B, S, N_HEADS * V_HEAD)


def _layer(x, ffn1_pre, ffn1_w_gu, ffn1_w_down, ffn1_post, mix_pre, w_in, b_gate,
           q_norm, kv_norm, w_uq, w_ukv, conv_w, w_conv_out, w_o, mix_post,
           ffn2_pre, ffn2_w_gu, ffn2_w_down, ffn2_post):
    h = x + 0.5 * _rms(_swiglu(_rms(x, ffn1_pre), ffn1_w_gu, ffn1_w_down), ffn1_post)
    u = _rms(h, mix_pre)
    z = u @ w_in
    c_q = z[..., OFF_CQ:OFF_CKV]
    c_kv = z[..., OFF_CKV:OFF_KR]
    k_rope = z[..., OFF_KR:OFF_CB]
    cb = z[..., OFF_CB:OFF_CC]
    cc = z[..., OFF_CC:OFF_CX]
    cx = z[..., OFF_CX:OFF_G]
    gates = jax.nn.sigmoid(z[..., OFF_G:] + b_gate)
    g_a, g_c = gates[..., :D_MODEL], gates[..., D_MODEL:]
    a = _mla(c_q, c_kv, k_rope, q_norm, kv_norm, w_uq, w_ukv)
    cv = (cb * _short_conv(cc * cx, conv_w)) @ w_conv_out
    m = (g_a * a + g_c * cv) @ w_o
    h = h + _rms(m, mix_post)
    return h + 0.5 * _rms(_swiglu(_rms(h, ffn2_pre), ffn2_w_gu, ffn2_w_down), ffn2_post)


def setup_inputs(seed: int = 0) -> dict:
    key = jax.random.key(seed)
    ks = jax.random.split(key, 24)
    f = jnp.float32

    def w(k, shape, fan_in):
        return jax.random.normal(k, (DEPTH,) + shape, f) * (fan_in ** -0.5)

    def gain(k, n):
        return 1.0 + 0.02 * jax.random.normal(k, (DEPTH, n), f)

    return {
        "x_prompt": jax.random.normal(ks[0], (BATCH, SEQ, D_MODEL), f),
        "x_sample": jax.random.normal(ks[1], (DEC_BATCH, DEC_SEQ, D_MODEL), f),
        "ffn1_pre": gain(ks[2], D_MODEL),
        "ffn1_w_gu": w(ks[3], (D_MODEL, 2 * D_FF), D_MODEL),
        "ffn1_w_down": w(ks[4], (D_FF, D_MODEL), D_FF),
        "ffn1_post": gain(ks[5], D_MODEL),
        "mix_pre": gain(ks[6], D_MODEL),
        "w_in": w(ks[7], (D_MODEL, D_IN), D_MODEL),
        "b_gate": 0.01 * jax.random.normal(ks[8], (DEPTH, N_BRANCH * D_MODEL), f),
        "q_norm": gain(ks[9], Q_LORA),
        "kv_norm": gain(ks[10], KV_LORA),
        "w_uq": w(ks[11], (Q_LORA, N_HEADS * (QK_NOPE + QK_ROPE)), Q_LORA),
        "w_ukv": w(ks[12], (KV_LORA, N_HEADS * (QK_NOPE + V_HEAD)), KV_LORA),
        "conv_w": w(ks[13], (CONV_W, D_CONV), CONV_W),
        "w_conv_out": w(ks[14], (D_CONV, D_MODEL), D_CONV),
        "w_o": w(ks[15], (D_MODEL, D_MODEL), D_MODEL),
        "mix_post": gain(ks[16], D_MODEL),
        "ffn2_pre": gain(ks[17], D_MODEL),
        "ffn2_w_gu": w(ks[18], (D_MODEL, 2 * D_FF), D_MODEL),
        "ffn2_w_down": w(ks[19], (D_FF, D_MODEL), D_FF),
        "ffn2_post": gain(ks[20], D_MODEL),
    }


def reference(x_prompt, x_sample, ffn1_pre, ffn1_w_gu, ffn1_w_down, ffn1_post, mix_pre,
              w_in, b_gate, q_norm, kv_norm, w_uq, w_ukv, conv_w, w_conv_out, w_o,
              mix_post, ffn2_pre, ffn2_w_gu, ffn2_w_down, ffn2_post):
    y_prompt = x_prompt
    y_sample = x_sample
    for l in range(DEPTH):
        p = (ffn1_pre[l], ffn1_w_gu[l], ffn1_w_down[l], ffn1_post[l], mix_pre[l], w_in[l],
             b_gate[l], q_norm[l], kv_norm[l], w_uq[l], w_ukv[l], conv_w[l], w_conv_out[l],
             w_o[l], mix_post[l], ffn2_pre[l], ffn2_w_gu[l], ffn2_w_down[l], ffn2_post[l])
        y_prompt = _layer(y_prompt, *p)
        y_sample = _layer(y_sample, *p)
    return (y_prompt, y_sample)
```

```python
import functools

import jax
import jax.numpy as jnp
from jax import lax
from jax.experimental import pallas as pl
from jax.experimental.pallas import tpu as pltpu

N_HEADS = 16
QK_NOPE = 128
QK_ROPE = 64
V_HEAD = 128
Q_LORA = 768
KV_LORA = 512
ROPE_THETA = 10000.0
D_CONV = 1024
EPS = 1e-6

QK_PAD = 256
LAT_PAD = 1408
KR_OFF = Q_LORA + KV_LORA
ROPE_HALF = QK_ROPE // 2
LANES = 128
NEG_BIG = -1e30

TM_FFN = 512
TF_FFN = 512
TM_IN = 512
TM_QKV = 256
TQ = 512
TK = 512
TM_MIX = 512
TC_MIX = 512
VMEM_LIMIT = 56 * 1024 * 1024

_NT = (((1,), (1,)), ((), ()))


def _rms(x, g):
    ms = jnp.mean(x * x, axis=-1, keepdims=True)
    return x * lax.rsqrt(ms + EPS) * g


def _sigmoid(x):
    return 1.0 / (1.0 + jnp.exp(-x))


def _dot(a, b):
    return jnp.dot(a, b, preferred_element_type=jnp.float32)


def _resident(shape):
    zeros = (0,) * len(shape)
    return pl.BlockSpec(shape, lambda *_: zeros, pipeline_mode=pl.Buffered(1))


def _ffn_kernel(x_ref, pre_ref, wg_ref, wu_ref, wd_ref, post_ref, o_ref, xn_ref, acc_ref):
    j = pl.program_id(1)

    @pl.when(j == 0)
    def _():
        xn_ref[...] = _rms(x_ref[...], pre_ref[...]).astype(jnp.bfloat16)

    xn = xn_ref[...]
    gate = _dot(xn, wg_ref[...])
    up = _dot(xn, wu_ref[...])
    act = (gate * _sigmoid(gate) * up).astype(jnp.bfloat16)
    part = _dot(act, wd_ref[...])

    @pl.when(j == 0)
    def _():
        acc_ref[...] = part

    @pl.when(j > 0)
    def _():
        acc_ref[...] += part

    @pl.when(j == pl.num_programs(1) - 1)
    def _():
        o_ref[...] = x_ref[...] + 0.5 * _rms(acc_ref[...], post_ref[...])


def _ffn(x, pre, w_gu, w_down, post):
    n, d = x.shape
    d_ff = w_down.shape[0]
    tm, tf = TM_FFN, TF_FFN
    nj = d_ff // tf
    return pl.pallas_call(
        _ffn_kernel,
        grid=(n // tm, nj),
        in_specs=[
            pl.BlockSpec((tm, d), lambda i, j: (i, 0)),
            pl.BlockSpec((1, d), lambda i, j: (0, 0)),
            pl.BlockSpec((d, tf), lambda i, j: (0, j)),
            pl.BlockSpec((d, tf), lambda i, j: (0, j + nj)),
            pl.BlockSpec((tf, d), lambda i, j: (j, 0)),
            pl.BlockSpec((1, d), lambda i, j: (0, 0)),
        ],
        out_specs=pl.BlockSpec((tm, d), lambda i, j: (i, 0)),
        out_shape=jax.ShapeDtypeStruct((n, d), jnp.float32),
        scratch_shapes=[pltpu.VMEM((tm, d), jnp.bfloat16), pltpu.VMEM((tm, d), jnp.float32)],
        compiler_params=pltpu.CompilerParams(
            dimension_semantics=("parallel", "arbitrary"), vmem_limit_bytes=VMEM_LIMIT),
        name="ffn",
    )(x, pre, w_gu, w_gu, w_down, post)


def _in_proj_kernel(h_ref, pre_ref, wlat_ref, wconv_ref, lat_ref, cb_ref, p_ref):
    u = _rms(h_ref[...], pre_ref[...]).astype(jnp.bfloat16)
    lat_ref[...] = _dot(u, wlat_ref[...])
    cb_ref[...] = _dot(u, wconv_ref[:, :D_CONV]).astype(jnp.bfloat16)
    cc = _dot(u, wconv_ref[:, D_CONV:2 * D_CONV])
    cx = _dot(u, wconv_ref[:, 2 * D_CONV:])
    p_ref[...] = (cc * cx).astype(jnp.bfloat16)


def _in_proj(h, pre, w_lat, w_conv):
    n, d = h.shape
    tm = TM_IN
    return pl.pallas_call(
        _in_proj_kernel,
        grid=(n // tm,),
        in_specs=[
            pl.BlockSpec((tm, d), lambda i: (i, 0)),
            _resident((1, d)),
            _resident(w_lat.shape),
            _resident(w_conv.shape),
        ],
        out_specs=[
            pl.BlockSpec((tm, LAT_PAD), lambda i: (i, 0)),
            pl.BlockSpec((tm, D_CONV), lambda i: (i, 0)),
            pl.BlockSpec((tm, D_CONV), lambda i: (i, 0)),
        ],
        out_shape=[
            jax.ShapeDtypeStruct((n, LAT_PAD), jnp.float32),
            jax.ShapeDtypeStruct((n, D_CONV), jnp.bfloat16),
            jax.ShapeDtypeStruct((n, D_CONV), jnp.bfloat16),
        ],
        compiler_params=pltpu.CompilerParams(
            dimension_semantics=("parallel",), vmem_limit_bytes=VMEM_LIMIT),
        name="in_proj",
    )(h, pre, w_lat, w_conv)


def _qkv_kernel(lat_ref, qn_ref, kvn_ref, wqt_ref, wuk_ref, wvt_ref, kc_ref, ks_ref,
                cost_ref, sint_ref, qt_ref, k_ref, vt_ref):
    tm = lat_ref.shape[0]
    qn = _rms(lat_ref[:, :Q_LORA], qn_ref[...]).astype(jnp.bfloat16)
    kvn = _rms(lat_ref[:, Q_LORA:KR_OFF], kvn_ref[...]).astype(jnp.bfloat16)

    kr = lat_ref[:, KR_OFF:]
    lane = lax.broadcasted_iota(jnp.int32, kr.shape, 1)
    swapped = jnp.where(lane < ROPE_HALF,
                        pltpu.roll(kr, LANES - ROPE_HALF, 1), pltpu.roll(kr, ROPE_HALF, 1))
    kro = (kr * kc_ref[...] + swapped * ks_ref[...]).astype(jnp.bfloat16)

    kn = _dot(kvn, wuk_ref[...]).astype(jnp.bfloat16)
    for h in range(N_HEADS):
        k_ref[0, h, :, :QK_NOPE] = kn[:, h * QK_NOPE:(h + 1) * QK_NOPE]
        k_ref[0, h, :, QK_NOPE:] = kro

    vt_ref[0, 0] = lax.dot_general(wvt_ref[...], kvn, _NT,
                                   preferred_element_type=jnp.float32).astype(jnp.bfloat16)

    scale = (QK_NOPE + QK_ROPE) ** -0.5
    qt = lax.dot_general(wqt_ref[...], qn, _NT, preferred_element_type=jnp.float32) * scale
    qt = qt.reshape(N_HEADS, QK_NOPE + QK_ROPE, tm)
    x1 = qt[:, QK_NOPE:QK_NOPE + ROPE_HALF, :]
    x2 = qt[:, QK_NOPE + ROPE_HALF:, :]
    cos = cost_ref[...][None]
    sin = sint_ref[...][None]
    qt_ref[0, :, :QK_NOPE, :] = qt[:, :QK_NOPE, :].astype(jnp.bfloat16)
    qt_ref[0, :, QK_NOPE:QK_NOPE + ROPE_HALF, :] = (x1 * cos - x2 * sin).astype(jnp.bfloat16)
    qt_ref[0, :, QK_NOPE + ROPE_HALF:QK_NOPE + QK_ROPE, :] = (x2 * cos + x1 * sin).astype(jnp.bfloat16)
    qt_ref[0, :, QK_NOPE + QK_ROPE:, :] = jnp.zeros(
        (N_HEADS, QK_PAD - QK_NOPE - QK_ROPE, tm), jnp.bfloat16)


def _qkv(lat, batch, seq, q_norm, kv_norm, w_qt, w_uk, w_vt, kcos, ksin, cos_t, sin_t):
    tm = TM_QKV
    ns = seq // tm
    per_chunk = TK // tm
    hv = N_HEADS * V_HEAD
    return pl.pallas_call(
        _qkv_kernel,
        grid=(batch, ns),
        in_specs=[
            pl.BlockSpec((tm, LAT_PAD), lambda b, s: (b * ns + s, 0)),
            _resident((1, Q_LORA)),
            _resident((1, KV_LORA)),
            _resident(w_qt.shape),
            _resident(w_uk.shape),
            _resident(w_vt.shape),
            pl.BlockSpec((tm, LANES), lambda b, s: (s, 0)),
            pl.BlockSpec((tm, LANES), lambda b, s: (s, 0)),
            pl.BlockSpec((ROPE_HALF, tm), lambda b, s: (0, s)),
            pl.BlockSpec((ROPE_HALF, tm), lambda b, s: (0, s)),
        ],
        out_specs=[
            pl.BlockSpec((1, N_HEADS, QK_PAD, tm), lambda b, s: (b, 0, 0, s)),
            pl.BlockSpec((1, N_HEADS, tm, QK_PAD), lambda b, s: (b, 0, s, 0)),
            pl.BlockSpec((1, 1, hv, tm), lambda b, s: (b, s // per_chunk, 0, s % per_chunk)),
        ],
        out_shape=[
            jax.ShapeDtypeStruct((batch, N_HEADS, QK_PAD, seq), jnp.bfloat16),
            jax.ShapeDtypeStruct((batch, N_HEADS, seq, QK_PAD), jnp.bfloat16),
            jax.ShapeDtypeStruct((batch, seq // TK, hv, TK), jnp.bfloat16),
        ],
        compiler_params=pltpu.CompilerParams(
            dimension_semantics=("parallel", "parallel"), vmem_limit_bytes=VMEM_LIMIT),
        name="qkv",
    )(lat, q_norm, kv_norm, w_qt, w_uk, w_vt, kcos, ksin, cos_t, sin_t)


def _attn_kernel(qt_ref, k_ref, vt_ref, o_ref, acc_ref):
    nk = k_ref.shape[2]
    tq = qt_ref.shape[3]
    qt = qt_ref[0, 0]
    acc_ref[...] = jnp.zeros_like(acc_ref)

    def step(c, carry):
        m_prev, l_prev = carry
        s = _dot(k_ref[0, 0, c], qt)
        m_new = jnp.maximum(m_prev, jnp.max(s, axis=0, keepdims=True))
        alpha = jnp.exp(m_prev - m_new)
        p = jnp.exp(s - m_new)
        l_new = alpha * l_prev + jnp.sum(p, axis=0, keepdims=True)
        acc_ref[...] = alpha * acc_ref[...] + _dot(vt_ref[0, c], p.astype(jnp.bfloat16))
        return m_new, l_new

    init = (jnp.full((1, tq), NEG_BIG, jnp.float32), jnp.zeros((1, tq), jnp.float32))
    _, l_fin = lax.fori_loop(0, nk, step, init)
    o_ref[0] = (acc_ref[...] / l_fin).T.astype(jnp.bfloat16)


def _attention(qt, k, vt):
    batch, _, _, seq = qt.shape
    nk = seq // TK
    k = k.reshape(batch, N_HEADS, nk, TK, QK_PAD)
    return pl.pallas_call(
        _attn_kernel,
        grid=(batch, N_HEADS, seq // TQ),
        in_specs=[
            pl.BlockSpec((1, 1, QK_PAD, TQ), lambda b, h, q: (b, h, 0, q)),
            pl.BlockSpec((1, 1, nk, TK, QK_PAD), lambda b, h, q: (b, h, 0, 0, 0)),
            pl.BlockSpec((1, nk, V_HEAD, TK), lambda b, h, q: (b, 0, h, 0)),
        ],
        out_specs=pl.BlockSpec((1, TQ, V_HEAD), lambda b, h, q: (b, q, h)),
        out_shape=jax.ShapeDtypeStruct((batch, seq, N_HEADS * V_HEAD), jnp.bfloat16),
        scratch_shapes=[pltpu.VMEM((V_HEAD, TQ), jnp.float32)],
        compiler_params=pltpu.CompilerParams(
            dimension_semantics=("parallel", "parallel", "arbitrary"),
            vmem_limit_bytes=VMEM_LIMIT),
        name="attention",
    )(qt, k, vt)


def _mix_kernel(tiles_per_seq, h_ref, a_ref, cb_ref, p_ref, pprev_ref, pnext_ref, pre_ref,
                wga_ref, wgc_ref, bga_ref, bgc_ref, cw_ref, wco_ref, wo_ref, post_ref,
                o_ref, u_ref, cvin_ref, acc_ref):
    i = pl.program_id(0)
    j = pl.program_id(1)
    tm = h_ref.shape[0]

    @pl.when(j == 0)
    def _():
        u_ref[...] = _rms(h_ref[...], pre_ref[...]).astype(jnp.bfloat16)
        p = p_ref[...].astype(jnp.float32)
        s_idx = i % tiles_per_seq
        prev_row = jnp.where(s_idx == 0, 0.0, pprev_ref[7:8, :].astype(jnp.float32))
        next_row = jnp.where(s_idx == tiles_per_seq - 1, 0.0,
                             pnext_ref[0:1, :].astype(jnp.float32))
        row = lax.broadcasted_iota(jnp.int32, p.shape, 0)
        p_before = jnp.where(row == 0, prev_row, pltpu.roll(p, 1, 0))
        p_after = jnp.where(row == tm - 1, next_row, pltpu.roll(p, tm - 1, 0))
        conv = p_before * cw_ref[0:1, :] + p * cw_ref[1:2, :] + p_after * cw_ref[2:3, :]
        cvin_ref[...] = (cb_ref[...].astype(jnp.float32) * conv).astype(jnp.bfloat16)

    u = u_ref[...]
    g_a = _sigmoid(_dot(u, wga_ref[...]) + bga_ref[...])
    g_c = _sigmoid(_dot(u, wgc_ref[...]) + bgc_ref[...])
    cv = _dot(cvin_ref[...], wco_ref[...])
    mixed = (g_a * a_ref[...].astype(jnp.float32) + g_c * cv).astype(jnp.bfloat16)
    part = _dot(mixed, wo_ref[...])

    @pl.when(j == 0)
    def _():
        acc_ref[...] = part

    @pl.when(j > 0)
    def _():
        acc_ref[...] += part

    @pl.when(j == pl.num_programs(1) - 1)
    def _():
        o_ref[...] = h_ref[...] + _rms(acc_ref[...], post_ref[...])


def _mix_out(h, a, cb, p, seq, pre, w_g, b_g, conv_w, w_co, w_o, post):
    n, d = h.shape
    tm, tc = TM_MIX, TC_MIX
    nj = d // tc
    halo = 8
    per_tile = tm // halo
    last_halo = n // halo - 1
    return pl.pallas_call(
        functools.partial(_mix_kernel, seq // tm),
        grid=(n // tm, nj),
        in_specs=[
            pl.BlockSpec((tm, d), lambda i, j: (i, 0)),
            pl.BlockSpec((tm, tc), lambda i, j: (i, j)),
            pl.BlockSpec((tm, D_CONV), lambda i, j: (i, 0)),
            pl.BlockSpec((tm, D_CONV), lambda i, j: (i, 0)),
            pl.BlockSpec((halo, D_CONV), lambda i, j: (jnp.maximum(i * per_tile - 1, 0), 0)),
            pl.BlockSpec((halo, D_CONV), lambda i, j: (jnp.minimum((i + 1) * per_tile, last_halo), 0)),
            pl.BlockSpec((1, d), lambda i, j: (0, 0)),
            pl.BlockSpec((d, tc), lambda i, j: (0, j)),
            pl.BlockSpec((d, tc), lambda i, j: (0, j + nj)),
            pl.BlockSpec((1, tc), lambda i, j: (0, j)),
            pl.BlockSpec((1, tc), lambda i, j: (0, j + nj)),
            pl.BlockSpec((3, D_CONV), lambda i, j: (0, 0)),
            pl.BlockSpec((D_CONV, tc), lambda i, j: (0, j)),
            pl.BlockSpec((tc, d), lambda i, j: (j, 0)),
            pl.BlockSpec((1, d), lambda i, j: (0, 0)),
        ],
        out_specs=pl.BlockSpec((tm, d), lambda i, j: (i, 0)),
        out_shape=jax.ShapeDtypeStruct((n, d), jnp.float32),
        scratch_shapes=[
            pltpu.VMEM((tm, d), jnp.bfloat16),
            pltpu.VMEM((tm, D_CONV), jnp.bfloat16),
            pltpu.VMEM((tm, d), jnp.float32),
        ],
        compiler_params=pltpu.CompilerParams(
            dimension_semantics=("parallel", "arbitrary"), vmem_limit_bytes=VMEM_LIMIT),
        name="mix_out",
    )(h, a, cb, p, p, p, pre, w_g, w_g, b_g, b_g, conv_w, w_co, w_o, post)


def _rope_tables(seq):
    pos = jnp.arange(seq, dtype=jnp.float32)
    inv_freq = ROPE_THETA ** (-jnp.arange(0, QK_ROPE, 2, dtype=jnp.float32) / QK_ROPE)
    ang = pos[:, None] * inv_freq[None, :]
    cos, sin = jnp.cos(ang), jnp.sin(ang)
    zeros = jnp.zeros((seq, LANES - QK_ROPE), jnp.float32)
    kcos = jnp.concatenate([cos, cos, zeros], axis=1)
    ksin = jnp.concatenate([-sin, sin, zeros], axis=1)
    return kcos, ksin, cos.T, sin.T


def _prep_weights(ffn1_pre, ffn1_w_gu, ffn1_w_down, ffn1_post, mix_pre, w_in, b_gate, q_norm,
                  kv_norm, w_uq, w_ukv, conv_w, w_conv_out, w_o, mix_post, ffn2_pre, ffn2_w_gu,
                  ffn2_w_down, ffn2_post):
    bf = jnp.bfloat16
    d = w_in.shape[0]
    row = lambda v: v.reshape(1, -1)
    off_cb = KR_OFF + QK_ROPE
    off_g = off_cb + 3 * D_CONV
    w_lat = jnp.concatenate(
        [w_in[:, :off_cb], jnp.zeros((d, LAT_PAD - off_cb), w_in.dtype)], axis=1).astype(bf)
    w_ukv3 = w_ukv.reshape(KV_LORA, N_HEADS, QK_NOPE + V_HEAD)
    return dict(
        ffn1=(row(ffn1_pre), ffn1_w_gu.astype(bf), ffn1_w_down.astype(bf), row(ffn1_post)),
        ffn2=(row(ffn2_pre), ffn2_w_gu.astype(bf), ffn2_w_down.astype(bf), row(ffn2_post)),
        mix_pre=row(mix_pre),
        w_lat=w_lat,
        w_conv=w_in[:, off_cb:off_g].astype(bf),
        w_g=w_in[:, off_g:].astype(bf),
        b_g=row(b_gate),
        q_norm=row(q_norm),
        kv_norm=row(kv_norm),
        w_qt=w_uq.T.astype(bf),
        w_uk=w_ukv3[:, :, :QK_NOPE].reshape(KV_LORA, N_HEADS * QK_NOPE).astype(bf),
        w_vt=w_ukv3[:, :, QK_NOPE:].reshape(KV_LORA, N_HEADS * V_HEAD).T.astype(bf),
        conv_w=conv_w,
        w_co=w_conv_out.astype(bf),
        w_o=w_o.astype(bf),
        mix_post=row(mix_post),
    )


def _layer(x, w, tables):
    batch, seq, d = x.shape
    x2 = x.reshape(batch * seq, d)
    h = _ffn(x2, *w["ffn1"])
    lat, cb, p = _in_proj(h, w["mix_pre"], w["w_lat"], w["w_conv"])
    qt, k, vt = _qkv(lat, batch, seq, w["q_norm"], w["kv_norm"], w["w_qt"], w["w_uk"],
                     w["w_vt"], *tables)
    a = _attention(qt, k, vt).reshape(batch * seq, d)
    h2 = _mix_out(h, a, cb, p, seq, w["mix_pre"], w["w_g"], w["b_g"], w["conv_w"],
                  w["w_co"], w["w_o"], w["mix_post"])
    y = _ffn(h2, *w["ffn2"])
    return y.reshape(batch, seq, d)


def kernel(x_prompt, x_sample, ffn1_pre, ffn1_w_gu, ffn1_w_down, ffn1_post, mix_pre, w_in, b_gate, q_norm, kv_norm, w_uq, w_ukv, conv_w, w_conv_out, w_o, mix_post, ffn2_pre, ffn2_w_gu, ffn2_w_down, ffn2_post):
    params = (ffn1_pre, ffn1_w_gu, ffn1_w_down, ffn1_post, mix_pre, w_in, b_gate, q_norm,
              kv_norm, w_uq, w_ukv, conv_w, w_conv_out, w_o, mix_post, ffn2_pre, ffn2_w_gu,
              ffn2_w_down, ffn2_post)
    depth = ffn1_pre.shape[0]
    tables = _rope_tables(max(x_prompt.shape[1], x_sample.shape[1]))
    y_prompt, y_sample = x_prompt, x_sample
    for l in range(depth):
        w = _prep_weights(*(t[l] for t in params))
        y_prompt = _layer(y_prompt, w, tables)
        y_sample = _layer(y_sample, w, tables)
    return (y_prompt, y_sample)
```

```python
import functools

import jax
import jax.numpy as jnp
from jax import lax
from jax.experimental import pallas as pl
from jax.experimental.pallas import tpu as pltpu

N_HEADS = 16
QK_NOPE = 128
QK_ROPE = 64
V_HEAD = 128
Q_LORA = 768
KV_LORA = 512
ROPE_THETA = 10000.0
D_CONV = 1024
EPS = 1e-6

QK_PAD = 256
LAT_PAD = 1408
KR_OFF = Q_LORA + KV_LORA
ROPE_HALF = QK_ROPE // 2
LANES = 128
NEG_BIG = -1e30
LOG2E = 1.4426950408889634

TM_FFN = 512
TF_FFN = 512
TM_IN = 512
TM_QKV = 256
TQ = 512
TK = 512
ATTN_GROUP = 4
TM_MIX = 512
TC_MIX = 512
VMEM_LIMIT = 56 * 1024 * 1024

_NT = (((1,), (1,)), ((), ()))


def _rms(x, g):
    ms = jnp.mean(x * x, axis=-1, keepdims=True)
    return x * lax.rsqrt(ms + EPS) * g


def _sigmoid(x):
    return 1.0 / (1.0 + jnp.exp(-x))


def _dot(a, b):
    return jnp.dot(a, b, preferred_element_type=jnp.float32)


def _resident(shape):
    zeros = (0,) * len(shape)
    return pl.BlockSpec(shape, lambda *_: zeros, pipeline_mode=pl.Buffered(1))


def _ffn_kernel(x_ref, pre_ref, wg_ref, wu_ref, wd_ref, post_ref, o_ref, xn_ref, acc_ref):
    j = pl.program_id(1)

    @pl.when(j == 0)
    def _():
        xn_ref[...] = _rms(x_ref[...], pre_ref[...]).astype(jnp.bfloat16)

    xn = xn_ref[...]
    gate = _dot(xn, wg_ref[...])
    up = _dot(xn, wu_ref[...])
    act = (gate * _sigmoid(gate) * up).astype(jnp.bfloat16)
    part = _dot(act, wd_ref[...])

    @pl.when(j == 0)
    def _():
        acc_ref[...] = part

    @pl.when(j > 0)
    def _():
        acc_ref[...] += part

    @pl.when(j == pl.num_programs(1) - 1)
    def _():
        o_ref[...] = x_ref[...] + 0.5 * _rms(acc_ref[...], post_ref[...])


def _ffn(x, pre, w_gu, w_down, post):
    n, d = x.shape
    d_ff = w_down.shape[0]
    tm, tf = TM_FFN, TF_FFN
    nj = d_ff // tf
    return pl.pallas_call(
        _ffn_kernel,
        grid=(n // tm, nj),
        in_specs=[
            pl.BlockSpec((tm, d), lambda i, j: (i, 0)),
            pl.BlockSpec((1, d), lambda i, j: (0, 0)),
            pl.BlockSpec((d, tf), lambda i, j: (0, j)),
            pl.BlockSpec((d, tf), lambda i, j: (0, j + nj)),
            pl.BlockSpec((tf, d), lambda i, j: (j, 0)),
            pl.BlockSpec((1, d), lambda i, j: (0, 0)),
        ],
        out_specs=pl.BlockSpec((tm, d), lambda i, j: (i, 0)),
        out_shape=jax.ShapeDtypeStruct((n, d), jnp.float32),
        scratch_shapes=[pltpu.VMEM((tm, d), jnp.bfloat16), pltpu.VMEM((tm, d), jnp.float32)],
        compiler_params=pltpu.CompilerParams(
            dimension_semantics=("parallel", "arbitrary"), vmem_limit_bytes=VMEM_LIMIT),
        name="ffn",
    )(x, pre, w_gu, w_gu, w_down, post)


def _in_proj_kernel(h_ref, pre_ref, wlat_ref, wconv_ref, lat_ref, cb_ref, p_ref):
    u = _rms(h_ref[...], pre_ref[...]).astype(jnp.bfloat16)
    lat_ref[...] = _dot(u, wlat_ref[...])
    cb_ref[...] = _dot(u, wconv_ref[:, :D_CONV]).astype(jnp.bfloat16)
    cc = _dot(u, wconv_ref[:, D_CONV:2 * D_CONV])
    cx = _dot(u, wconv_ref[:, 2 * D_CONV:])
    p_ref[...] = (cc * cx).astype(jnp.bfloat16)


def _in_proj(h, pre, w_lat, w_conv):
    n, d = h.shape
    tm = TM_IN
    return pl.pallas_call(
        _in_proj_kernel,
        grid=(n // tm,),
        in_specs=[
            pl.BlockSpec((tm, d), lambda i: (i, 0)),
            _resident((1, d)),
            _resident(w_lat.shape),
            _resident(w_conv.shape),
        ],
        out_specs=[
            pl.BlockSpec((tm, LAT_PAD), lambda i: (i, 0)),
            pl.BlockSpec((tm, D_CONV), lambda i: (i, 0)),
            pl.BlockSpec((tm, D_CONV), lambda i: (i, 0)),
        ],
        out_shape=[
            jax.ShapeDtypeStruct((n, LAT_PAD), jnp.float32),
            jax.ShapeDtypeStruct((n, D_CONV), jnp.bfloat16),
            jax.ShapeDtypeStruct((n, D_CONV), jnp.bfloat16),
        ],
        compiler_params=pltpu.CompilerParams(
            dimension_semantics=("parallel",), vmem_limit_bytes=VMEM_LIMIT),
        name="in_proj",
    )(h, pre, w_lat, w_conv)


def _qkv_kernel(lat_ref, qn_ref, kvn_ref, wqt_ref, wuk_ref, wvt_ref, kc_ref, ks_ref,
                cost_ref, sint_ref, qt_ref, k_ref, vt_ref):
    tm = lat_ref.shape[0]
    qn = _rms(lat_ref[:, :Q_LORA], qn_ref[...]).astype(jnp.bfloat16)
    kvn = _rms(lat_ref[:, Q_LORA:KR_OFF], kvn_ref[...]).astype(jnp.bfloat16)

    kr = lat_ref[:, KR_OFF:]
    lane = lax.broadcasted_iota(jnp.int32, kr.shape, 1)
    swapped = jnp.where(lane < ROPE_HALF,
                        pltpu.roll(kr, LANES - ROPE_HALF, 1), pltpu.roll(kr, ROPE_HALF, 1))
    kro = (kr * kc_ref[...] + swapped * ks_ref[...]).astype(jnp.bfloat16)

    kn = _dot(kvn, wuk_ref[...]).astype(jnp.bfloat16)
    for h in range(N_HEADS):
        k_ref[0, h, :, :QK_NOPE] = kn[:, h * QK_NOPE:(h + 1) * QK_NOPE]
        k_ref[0, h, :, QK_NOPE:] = kro

    vt_ref[0, 0] = lax.dot_general(wvt_ref[...], kvn, _NT,
                                   preferred_element_type=jnp.float32).astype(jnp.bfloat16)

    scale = (QK_NOPE + QK_ROPE) ** -0.5 * LOG2E
    qt =lax.dot_general(wqt_ref[...], qn, _NT, preferred_element_type=jnp.float32) * scale
    qt = qt.reshape(N_HEADS, QK_NOPE + QK_ROPE, tm)
    x1 = qt[:, QK_NOPE:QK_NOPE + ROPE_HALF, :]
    x2 = qt[:, QK_NOPE + ROPE_HALF:, :]
    cos = cost_ref[...][None]
    sin = sint_ref[...][None]
    qt_ref[0, :, :QK_NOPE, :] = qt[:, :QK_NOPE, :].astype(jnp.bfloat16)
    qt_ref[0, :, QK_NOPE:QK_NOPE + ROPE_HALF, :] = (x1 * cos - x2 * sin).astype(jnp.bfloat16)
    qt_ref[0, :, QK_NOPE + ROPE_HALF:QK_NOPE + QK_ROPE, :] = (x2 * cos + x1 * sin).astype(jnp.bfloat16)
    qt_ref[0, :, QK_NOPE + QK_ROPE:, :] = jnp.zeros(
        (N_HEADS, QK_PAD - QK_NOPE - QK_ROPE, tm), jnp.bfloat16)


def _qkv(lat, batch, seq, q_norm, kv_norm, w_qt, w_uk, w_vt, kcos, ksin, cos_t, sin_t):
    tm = TM_QKV
    ns = seq // tm
    per_chunk = TK // tm
    hv = N_HEADS * V_HEAD
    return pl.pallas_call(
        _qkv_kernel,
        grid=(batch, ns),
        in_specs=[
            pl.BlockSpec((tm, LAT_PAD), lambda b, s: (b * ns + s, 0)),
            _resident((1, Q_LORA)),
            _resident((1, KV_LORA)),
            _resident(w_qt.shape),
            _resident(w_uk.shape),
            _resident(w_vt.shape),
            pl.BlockSpec((tm, LANES), lambda b, s: (s, 0)),
            pl.BlockSpec((tm, LANES), lambda b, s: (s, 0)),
            pl.BlockSpec((ROPE_HALF, tm), lambda b, s: (0, s)),
            pl.BlockSpec((ROPE_HALF, tm), lambda b, s: (0, s)),
        ],
        out_specs=[
            pl.BlockSpec((1, N_HEADS, QK_PAD, tm), lambda b, s: (b, 0, 0, s)),
            pl.BlockSpec((1, N_HEADS, tm, QK_PAD), lambda b, s: (b, 0, s, 0)),
            pl.BlockSpec((1, 1, hv, tm), lambda b, s: (b, s // per_chunk, 0, s % per_chunk)),
        ],
        out_shape=[
            jax.ShapeDtypeStruct((batch, N_HEADS, QK_PAD, seq), jnp.bfloat16),
            jax.ShapeDtypeStruct((batch, N_HEADS, seq, QK_PAD), jnp.bfloat16),
            jax.ShapeDtypeStruct((batch, seq // TK, hv, TK), jnp.bfloat16),
        ],
        compiler_params=pltpu.CompilerParams(
            dimension_semantics=("parallel", "parallel"), vmem_limit_bytes=VMEM_LIMIT),
        name="qkv",
    )(lat, q_norm, kv_norm, w_qt, w_uk, w_vt, kcos, ksin, cos_t, sin_t)


def _attn_kernel(qt_ref, k_ref, vt_ref, o_ref, s0_ref, s1_ref, p0_ref, p1_ref, acc_ref):
    nk = k_ref.shape[2]
    tq = qt_ref.shape[3]
    qt = qt_ref[0, 0]

    def scores(c, s_ref):
        s_ref[...] = _dot(k_ref[0, 0, c], qt)

    def softmax(s_ref, p_ref, m_prev, l_prev):
        s = s_ref[...]
        m_new = jnp.maximum(m_prev, jnp.max(s, axis=0, keepdims=True))
        alpha = jnp.exp2(m_prev - m_new)
        p = jnp.exp2(s - m_new)
        p_ref[...] = p.astype(jnp.bfloat16)
        return m_new, alpha * l_prev + jnp.sum(p, axis=0, keepdims=True), alpha

    def apply(c, p_ref, alpha):
        acc_ref[...] = alpha * acc_ref[...] + _dot(vt_ref[0, c], p_ref[...])

    s_refs = (s0_ref, s1_ref)
    p_refs = (p0_ref, p1_ref)
    group_len = ATTN_GROUP if nk % ATTN_GROUP == 0 else 2

    def group(i, carry, first, last):
        m, l, alpha = carry
        for t in range(group_len):
            c = group_len * i + t
            if not (last and t == group_len - 1):
                scores(c + 1, s_refs[1 - t % 2])
            if not (first and t == 0):
                apply(c - 1, p_refs[1 - t % 2], alpha)
            m, l, alpha = softmax(s_refs[t % 2], p_refs[t % 2], m, l)
        return m, l, alpha

    acc_ref[...] = jnp.zeros_like(acc_ref)
    scores(0, s0_ref)
    carry = (jnp.full((1, tq), NEG_BIG, jnp.float32), jnp.zeros((1, tq), jnp.float32),
             jnp.ones((1, tq), jnp.float32))
    n_groups = nk // group_len
    if n_groups == 1:
        carry = group(0, carry, True, True)
    else:
        carry = group(0, carry, True, False)
        carry = lax.fori_loop(1, n_groups - 1, lambda i, c: group(i, c, False, False), carry)
        carry = group(n_groups - 1, carry, False, True)
    _, l_fin, alpha = carry
    apply(nk - 1, p1_ref, alpha)
    o_ref[0] = (acc_ref[...] / l_fin).T.astype(jnp.bfloat16)


def _attention(qt, k, vt):
    batch, _, _, seq = qt.shape
    nk = seq // TK
    assert seq % TQ == 0 and nk % 2 == 0, (seq, TQ, TK)
    k = k.reshape(batch, N_HEADS, nk, TK, QK_PAD)
    return pl.pallas_call(
        _attn_kernel,
        grid=(batch, N_HEADS, seq // TQ),
        in_specs=[
            pl.BlockSpec((1, 1, QK_PAD, TQ), lambda b, h, q: (b, h, 0, q)),
            pl.BlockSpec((1, 1, nk, TK, QK_PAD), lambda b, h, q: (b, h, 0, 0, 0)),
            pl.BlockSpec((1, nk, V_HEAD, TK), lambda b, h, q: (b, 0, h, 0)),
        ],
        out_specs=pl.BlockSpec((1, TQ, V_HEAD), lambda b, h, q: (b, q, h)),
        out_shape=jax.ShapeDtypeStruct((batch, seq, N_HEADS * V_HEAD), jnp.bfloat16),
        scratch_shapes=[
            pltpu.VMEM((TK, TQ), jnp.float32), pltpu.VMEM((TK, TQ), jnp.float32),
            pltpu.VMEM((TK, TQ), jnp.bfloat16), pltpu.VMEM((TK, TQ), jnp.bfloat16),
            pltpu.VMEM((V_HEAD, TQ), jnp.float32),
        ],
        compiler_params=pltpu.CompilerParams(
            dimension_semantics=("parallel", "parallel", "arbitrary"),
            vmem_limit_bytes=VMEM_LIMIT),
        name="attention",
    )(qt, k, vt)


def _mix_kernel(tiles_per_seq, h_ref, a_ref, cb_ref, p_ref, pprev_ref, pnext_ref, pre_ref,
                wga_ref, wgc_ref, bga_ref, bgc_ref, cw_ref, wco_ref, wo_ref, post_ref,
                o_ref, u_ref, cvin_ref, acc_ref):
    i = pl.program_id(0)
    j = pl.program_id(1)
    tm = h_ref.shape[0]

    @pl.when(j == 0)
    def _():
        u_ref[...] = _rms(h_ref[...], pre_ref[...]).astype(jnp.bfloat16)
        p = p_ref[...].astype(jnp.float32)
        s_idx = i % tiles_per_seq
        prev_row = jnp.where(s_idx == 0, 0.0, pprev_ref[7:8, :].astype(jnp.float32))
        next_row = jnp.where(s_idx == tiles_per_seq - 1, 0.0,
                             pnext_ref[0:1, :].astype(jnp.float32))
        row = lax.broadcasted_iota(jnp.int32, p.shape, 0)
        p_before = jnp.where(row == 0, prev_row, pltpu.roll(p, 1, 0))
        p_after = jnp.where(row == tm - 1, next_row, pltpu.roll(p, tm - 1, 0))
        conv = p_before * cw_ref[0:1, :] + p * cw_ref[1:2, :] + p_after * cw_ref[2:3, :]
        cvin_ref[...] = (cb_ref[...].astype(jnp.float32) * conv).astype(jnp.bfloat16)

    u = u_ref[...]
    g_a = _sigmoid(_dot(u, wga_ref[...]) + bga_ref[...])
    g_c = _sigmoid(_dot(u, wgc_ref[...]) + bgc_ref[...])
    cv = _dot(cvin_ref[...], wco_ref[...])
    mixed = (g_a * a_ref[...].astype(jnp.float32) + g_c * cv).astype(jnp.bfloat16)
    part = _dot(mixed, wo_ref[...])

    @pl.when(j == 0)
    def _():
        acc_ref[...] = part

    @pl.when(j > 0)
    def _():
        acc_ref[...] += part

    @pl.when(j == pl.num_programs(1) - 1)
    def _():
        o_ref[...] = h_ref[...] + _rms(acc_ref[...], post_ref[...])


def _mix_out(h, a, cb, p, seq, pre, w_g, b_g, conv_w, w_co, w_o, post):
    n, d = h.shape
    tm, tc = TM_MIX, TC_MIX
    nj = d // tc
    halo = 8
    per_tile = tm // halo
    last_halo = n // halo - 1
    return pl.pallas_call(
        functools.partial(_mix_kernel, seq // tm),
        grid=(n // tm, nj),
        in_specs=[
            pl.BlockSpec((tm, d), lambda i, j: (i, 0)),
            pl.BlockSpec((tm, tc), lambda i, j: (i, j)),
            pl.BlockSpec((tm, D_CONV), lambda i, j: (i, 0)),
            pl.BlockSpec((tm, D_CONV), lambda i, j: (i, 0)),
            pl.BlockSpec((halo, D_CONV), lambda i, j: (jnp.maximum(i * per_tile - 1, 0), 0)),
            pl.BlockSpec((halo, D_CONV), lambda i, j: (jnp.minimum((i + 1) * per_tile, last_halo), 0)),
            pl.BlockSpec((1, d), lambda i, j: (0, 0)),
            pl.BlockSpec((d, tc), lambda i, j: (0, j)),
            pl.BlockSpec((d, tc), lambda i, j: (0, j + nj)),
            pl.BlockSpec((1, tc), lambda i, j: (0, j)),
            pl.BlockSpec((1, tc), lambda i, j: (0, j + nj)),
            pl.BlockSpec((3, D_CONV), lambda i, j: (0, 0)),
            pl.BlockSpec((D_CONV, tc), lambda i, j: (0, j)),
            pl.BlockSpec((tc, d), lambda i, j: (j, 0)),
            pl.BlockSpec((1, d), lambda i, j: (0, 0)),
        ],
        out_specs=pl.BlockSpec((tm, d), lambda i, j: (i, 0)),
        out_shape=jax.ShapeDtypeStruct((n, d), jnp.float32),
        scratch_shapes=[
            pltpu.VMEM((tm, d), jnp.bfloat16),
            pltpu.VMEM((tm, D_CONV), jnp.bfloat16),
            pltpu.VMEM((tm, d), jnp.float32),
        ],
        compiler_params=pltpu.CompilerParams(
            dimension_semantics=("parallel", "arbitrary"), vmem_limit_bytes=VMEM_LIMIT),
        name="mix_out",
    )(h, a, cb, p, p, p, pre, w_g, w_g, b_g, b_g, conv_w, w_co, w_o, post)


def _rope_tables(seq):
    pos = jnp.arange(seq, dtype=jnp.float32)
    inv_freq = ROPE_THETA ** (-jnp.arange(0, QK_ROPE, 2, dtype=jnp.float32) / QK_ROPE)
    ang = pos[:, None] * inv_freq[None, :]
    cos, sin = jnp.cos(ang), jnp.sin(ang)
    zeros = jnp.zeros((seq, LANES - QK_ROPE), jnp.float32)
    kcos = jnp.concatenate([cos, cos, zeros], axis=1)
    ksin = jnp.concatenate([-sin, sin, zeros], axis=1)
    return kcos, ksin, cos.T, sin.T


def _prep_weights(ffn1_pre, ffn1_w_gu, ffn1_w_down, ffn1_post, mix_pre, w_in, b_gate, q_norm,
                  kv_norm, w_uq, w_ukv, conv_w, w_conv_out, w_o, mix_post, ffn2_pre, ffn2_w_gu,
                  ffn2_w_down, ffn2_post):
    bf = jnp.bfloat16
    d = w_in.shape[0]
    row = lambda v: v.reshape(1, -1)
    off_cb = KR_OFF + QK_ROPE
    off_g = off_cb + 3 * D_CONV
    w_lat = jnp.concatenate(
        [w_in[:, :off_cb], jnp.zeros((d, LAT_PAD - off_cb), w_in.dtype)], axis=1).astype(bf)
    w_ukv3 = w_ukv.reshape(KV_LORA, N_HEADS, QK_NOPE + V_HEAD)
    return dict(
        ffn1=(row(ffn1_pre), ffn1_w_gu.astype(bf), ffn1_w_down.astype(bf), row(ffn1_post)),
        ffn2=(row(ffn2_pre), ffn2_w_gu.astype(bf), ffn2_w_down.astype(bf), row(ffn2_post)),
        mix_pre=row(mix_pre),
        w_lat=w_lat,
        w_conv=w_in[:, off_cb:off_g].astype(bf),
        w_g=w_in[:, off_g:].astype(bf),
        b_g=row(b_gate),
        q_norm=row(q_norm),
        kv_norm=row(kv_norm),
        w_qt=w_uq.T.astype(bf),
        w_uk=w_ukv3[:, :, :QK_NOPE].reshape(KV_LORA, N_HEADS * QK_NOPE).astype(bf),
        w_vt=w_ukv3[:, :, QK_NOPE:].reshape(KV_LORA, N_HEADS * V_HEAD).T.astype(bf),
        conv_w=conv_w,
        w_co=w_conv_out.astype(bf),
        w_o=w_o.astype(bf),
        mix_post=row(mix_post),
    )


def _layer(x, w, tables):
    batch, seq, d = x.shape
    x2 = x.reshape(batch * seq, d)
    h = _ffn(x2, *w["ffn1"])
    lat, cb, p = _in_proj(h, w["mix_pre"], w["w_lat"], w["w_conv"])
    qt, k, vt = _qkv(lat, batch, seq, w["q_norm"], w["kv_norm"], w["w_qt"], w["w_uk"],
                     w["w_vt"], *tables)
    a = _attention(qt, k, vt).reshape(batch * seq, d)
    h2 = _mix_out(h, a, cb, p, seq, w["mix_pre"], w["w_g"], w["b_g"], w["conv_w"],
                  w["w_co"], w["w_o"], w["mix_post"])
    y = _ffn(h2, *w["ffn2"])
    return y.reshape(batch, seq, d)


def kernel(x_prompt, x_sample, ffn1_pre, ffn1_w_gu, ffn1_w_down, ffn1_post, mix_pre, w_in, b_gate, q_norm, kv_norm, w_uq, w_ukv, conv_w, w_conv_out, w_o, mix_post, ffn2_pre, ffn2_w_gu, ffn2_w_down, ffn2_post):
    params = (ffn1_pre, ffn1_w_gu, ffn1_w_down, ffn1_post, mix_pre, w_in, b_gate, q_norm,
              kv_norm, w_uq, w_ukv, conv_w, w_conv_out, w_o, mix_post, ffn2_pre, ffn2_w_gu,
              ffn2_w_down, ffn2_post)
    depth = ffn1_pre.shape[0]
    tables = _rope_tables(max(x_prompt.shape[1], x_sample.shape[1]))
    y_prompt, y_sample = x_prompt, x_sample
    for l in range(depth):
        w = _prep_weights(*(t[l] for t in params))
        y_prompt = _layer(y_prompt, w, tables)
        y_sample = _layer(y_sample, w, tables)
    return (y_prompt, y_sample)
```

```python
import functools

import jax
import jax.numpy as jnp
from jax import lax
from jax.experimental import pallas as pl
from jax.experimental.pallas import tpu as pltpu

N_HEADS = 16
QK_NOPE = 128
QK_ROPE = 64
V_HEAD = 128
Q_LORA = 768
KV_LORA = 512
ROPE_THETA = 10000.0
D_CONV = 1024
EPS = 1e-6

QK_PAD = 256
LAT_PAD = 1408
KR_OFF = Q_LORA + KV_LORA
ROPE_HALF = QK_ROPE // 2
LANES = 128
NEG_BIG = -1e30
LOG2E = 1.4426950408889634

TM_FFN = 512
TF_FFN = 512
TM_IN = 512
TM_QKV = 256
TQ = 512
TK = 512
ATTN_GROUP = 4
TM_MIX = 512
TC_MIX = 512
VMEM_LIMIT = 56 * 1024 * 1024

_NT = (((1,), (1,)), ((), ()))


def _rms(x, g):
    ms = jnp.mean(x * x, axis=-1, keepdims=True)
    return x * lax.rsqrt(ms + EPS) * g


def _sigmoid(x):
    return 1.0 / (1.0 + jnp.exp(-x))


def _dot(a, b):
    return jnp.dot(a, b, preferred_element_type=jnp.float32)


def _resident(shape):
    zeros = (0,) * len(shape)
    return pl.BlockSpec(shape, lambda *_: zeros, pipeline_mode=pl.Buffered(1))


def _ffn_kernel(x_ref, pre_ref, wg_ref, wu_ref, wd_ref, post_ref, o_ref, xn_ref, acc_ref):
    j = pl.program_id(1)

    @pl.when(j == 0)
    def _():
        xn_ref[...] = _rms(x_ref[...], pre_ref[...]).astype(jnp.bfloat16)
        acc_ref[...] = jnp.zeros_like(acc_ref)

    xn = xn_ref[...]
    gate = _dot(xn, wg_ref[...])
    up = _dot(xn, wu_ref[...])
    act = (gate * _sigmoid(gate) * up).astype(jnp.bfloat16)
    acc_ref[...] += _dot(act, wd_ref[...])

    @pl.when(j == pl.num_programs(1) - 1)
    def _():
        o_ref[...] = x_ref[...] + 0.5 * _rms(acc_ref[...], post_ref[...])


def _ffn(x, pre, w_gu, w_down, post):
    n, d = x.shape
    d_ff = w_down.shape[0]
    tm, tf = TM_FFN, TF_FFN
    nj = d_ff // tf
    return pl.pallas_call(
        _ffn_kernel,
        grid=(n // tm, nj),
        in_specs=[
            pl.BlockSpec((tm, d), lambda i, j: (i, 0)),
            pl.BlockSpec((1, d), lambda i, j: (0, 0)),
            pl.BlockSpec((d, tf), lambda i, j: (0, j)),
            pl.BlockSpec((d, tf), lambda i, j: (0, j + nj)),
            pl.BlockSpec((tf, d), lambda i, j: (j, 0)),
            pl.BlockSpec((1, d), lambda i, j: (0, 0)),
        ],
        out_specs=pl.BlockSpec((tm, d), lambda i, j: (i, 0)),
        out_shape=jax.ShapeDtypeStruct((n, d), jnp.float32),
        scratch_shapes=[pltpu.VMEM((tm, d), jnp.bfloat16), pltpu.VMEM((tm, d), jnp.float32)],
        compiler_params=pltpu.CompilerParams(
            dimension_semantics=("parallel", "arbitrary"), vmem_limit_bytes=VMEM_LIMIT),
        name="ffn",
    )(x, pre, w_gu, w_gu, w_down, post)


def _in_proj_kernel(h_ref, pre_ref, wlat_ref, wconv_ref, lat_ref, cb_ref, p_ref):
    u = _rms(h_ref[...], pre_ref[...]).astype(jnp.bfloat16)
    lat_ref[...] = _dot(u, wlat_ref[...])
    cb_ref[...] = _dot(u, wconv_ref[:, :D_CONV]).astype(jnp.bfloat16)
    cc = _dot(u, wconv_ref[:, D_CONV:2 * D_CONV])
    cx = _dot(u, wconv_ref[:, 2 * D_CONV:])
    p_ref[...] = (cc * cx).astype(jnp.bfloat16)


def _in_proj(h, pre, w_lat, w_conv):
    n, d = h.shape
    tm = TM_IN
    return pl.pallas_call(
        _in_proj_kernel,
        grid=(n // tm,),
        in_specs=[
            pl.BlockSpec((tm, d), lambda i: (i, 0)),
            _resident((1, d)),
            _resident(w_lat.shape),
            _resident(w_conv.shape),
        ],
        out_specs=[
            pl.BlockSpec((tm, LAT_PAD), lambda i: (i, 0)),
            pl.BlockSpec((tm, D_CONV), lambda i: (i, 0)),
            pl.BlockSpec((tm, D_CONV), lambda i: (i, 0)),
        ],
        out_shape=[
            jax.ShapeDtypeStruct((n, LAT_PAD), jnp.float32),
            jax.ShapeDtypeStruct((n, D_CONV), jnp.bfloat16),
            jax.ShapeDtypeStruct((n, D_CONV), jnp.bfloat16),
        ],
        compiler_params=pltpu.CompilerParams(
            dimension_semantics=("parallel",), vmem_limit_bytes=VMEM_LIMIT),
        name="in_proj",
    )(h, pre, w_lat, w_conv)


def _qkv_kernel(lat_ref, qn_ref, kvn_ref, wqt_ref, wuk_ref, wvt_ref, kc_ref, ks_ref,
                cost_ref, sint_ref, qt_ref, k_ref, vt_ref):
    tm = lat_ref.shape[0]
    qn = _rms(lat_ref[:, :Q_LORA], qn_ref[...]).astype(jnp.bfloat16)
    kvn = _rms(lat_ref[:, Q_LORA:KR_OFF], kvn_ref[...]).astype(jnp.bfloat16)

    kr = lat_ref[:, KR_OFF:]
    lane = lax.broadcasted_iota(jnp.int32, kr.shape, 1)
    swapped = jnp.where(lane < ROPE_HALF,
                        pltpu.roll(kr, LANES - ROPE_HALF, 1), pltpu.roll(kr, ROPE_HALF, 1))
    kro = (kr * kc_ref[...] + swapped * ks_ref[...]).astype(jnp.bfloat16)

    kn = _dot(kvn, wuk_ref[...]).astype(jnp.bfloat16)
    for h in range(N_HEADS):
        k_ref[0, h, :, :QK_NOPE] = kn[:, h * QK_NOPE:(h + 1) * QK_NOPE]
        k_ref[0, h, :, QK_NOPE:] = kro

    vt_ref[0, 0] = lax.dot_general(wvt_ref[...], kvn, _NT,
                                   preferred_element_type=jnp.float32).astype(jnp.bfloat16)

    scale = (QK_NOPE + QK_ROPE) ** -0.5 * LOG2E
    qt =lax.dot_general(wqt_ref[...], qn, _NT, preferred_element_type=jnp.float32) * scale
    qt = qt.reshape(N_HEADS, QK_NOPE + QK_ROPE, tm)
    x1 = qt[:, QK_NOPE:QK_NOPE + ROPE_HALF, :]
    x2 = qt[:, QK_NOPE + ROPE_HALF:, :]
    cos = cost_ref[...][None]
    sin = sint_ref[...][None]
    qt_ref[0, :, :QK_NOPE, :] = qt[:, :QK_NOPE, :].astype(jnp.bfloat16)
    qt_ref[0, :, QK_NOPE:QK_NOPE + ROPE_HALF, :] = (x1 * cos - x2 * sin).astype(jnp.bfloat16)
    qt_ref[0, :, QK_NOPE + ROPE_HALF:QK_NOPE + QK_ROPE, :] = (x2 * cos + x1 * sin).astype(jnp.bfloat16)
    qt_ref[0, :, QK_NOPE + QK_ROPE:, :] = jnp.zeros(
        (N_HEADS, QK_PAD - QK_NOPE - QK_ROPE, tm), jnp.bfloat16)


def _qkv(lat, batch, seq, q_norm, kv_norm, w_qt, w_uk, w_vt, kcos, ksin, cos_t, sin_t):
    tm = TM_QKV
    ns = seq // tm
    per_chunk = TK // tm
    hv = N_HEADS * V_HEAD
    return pl.pallas_call(
        _qkv_kernel,
        grid=(batch, ns),
        in_specs=[
            pl.BlockSpec((tm, LAT_PAD), lambda b, s: (b * ns + s, 0)),
            _resident((1, Q_LORA)),
            _resident((1, KV_LORA)),
            _resident(w_qt.shape),
            _resident(w_uk.shape),
            _resident(w_vt.shape),
            pl.BlockSpec((tm, LANES), lambda b, s: (s, 0)),
            pl.BlockSpec((tm, LANES), lambda b, s: (s, 0)),
            pl.BlockSpec((ROPE_HALF, tm), lambda b, s: (0, s)),
            pl.BlockSpec((ROPE_HALF, tm), lambda b, s: (0, s)),
        ],
        out_specs=[
            pl.BlockSpec((1, N_HEADS, QK_PAD, tm), lambda b, s: (b, 0, 0, s)),
            pl.BlockSpec((1, N_HEADS, tm, QK_PAD), lambda b, s: (b, 0, s, 0)),
            pl.BlockSpec((1, 1, hv, tm), lambda b, s: (b, s // per_chunk, 0, s % per_chunk)),
        ],
        out_shape=[
            jax.ShapeDtypeStruct((batch, N_HEADS, QK_PAD, seq), jnp.bfloat16),
            jax.ShapeDtypeStruct((batch, N_HEADS, seq, QK_PAD), jnp.bfloat16),
            jax.ShapeDtypeStruct((batch, seq // TK, hv, TK), jnp.bfloat16),
        ],
        compiler_params=pltpu.CompilerParams(
            dimension_semantics=("parallel", "parallel"), vmem_limit_bytes=VMEM_LIMIT),
        name="qkv",
    )(lat, q_norm, kv_norm, w_qt, w_uk, w_vt, kcos, ksin, cos_t, sin_t)


def _attn_kernel(qt_ref, k_ref, vt_ref, o_ref, s0_ref, s1_ref, p0_ref, p1_ref, acc_ref):
    nk = k_ref.shape[2]
    tq = qt_ref.shape[3]
    qt = qt_ref[0, 0]

    def scores(c, s_ref):
        s = _dot(k_ref[0, 0, c], qt)
        s_ref[...] = s
        return jnp.max(s, axis=0, keepdims=True)

    def softmax(s_ref, p_ref, s_max, m_prev, l_prev):
        m_new = jnp.maximum(m_prev, s_max)
        alpha = jnp.exp2(m_prev - m_new)
        p = jnp.exp2(s_ref[...] - m_new)
        p_ref[...] = p.astype(jnp.bfloat16)
        return m_new, alpha * l_prev + jnp.sum(p, axis=0, keepdims=True), alpha

    def apply(c, p_ref, alpha):
        acc_ref[...] = alpha * acc_ref[...] + _dot(vt_ref[0, c], p_ref[...])

    s_refs = (s0_ref, s1_ref)
    p_refs = (p0_ref, p1_ref)
    group_len = ATTN_GROUP if nk % ATTN_GROUP == 0 else 2

    def group(i, carry, first, last):
        m, l, alpha, s_max = carry
        for t in range(group_len):
            c = group_len * i + t
            s_max_next = s_max
            if not (last and t == group_len - 1):
                s_max_next = scores(c + 1, s_refs[1 - t % 2])
            if not (first and t == 0):
                apply(c - 1, p_refs[1 - t % 2], alpha)
            m, l, alpha = softmax(s_refs[t % 2], p_refs[t % 2], s_max, m, l)
            s_max = s_max_next
        return m, l, alpha, s_max

    acc_ref[...] = jnp.zeros_like(acc_ref)
    s_max0 = scores(0, s0_ref)
    carry = (jnp.full((1, tq), NEG_BIG, jnp.float32), jnp.zeros((1, tq), jnp.float32),
             jnp.ones((1, tq), jnp.float32), s_max0)
    n_groups = nk // group_len
    if n_groups == 1:
        carry = group(0, carry, True, True)
    else:
        carry = group(0, carry, True, False)
        carry = lax.fori_loop(1, n_groups - 1, lambda i, c: group(i, c, False, False), carry)
        carry = group(n_groups - 1, carry, False, True)
    _, l_fin, alpha, _ = carry
    apply(nk - 1, p1_ref, alpha)
    o_ref[0] = (acc_ref[...] / l_fin).T.astype(jnp.bfloat16)


def _attention(qt, k, vt):
    batch, _, _, seq = qt.shape
    nk = seq // TK
    assert seq % TQ == 0 and nk % 2 == 0, (seq, TQ, TK)
    k = k.reshape(batch, N_HEADS, nk, TK, QK_PAD)
    return pl.pallas_call(
        _attn_kernel,
        grid=(batch, N_HEADS, seq // TQ),
        in_specs=[
            pl.BlockSpec((1, 1, QK_PAD, TQ), lambda b, h, q: (b, h, 0, q)),
            pl.BlockSpec((1, 1, nk, TK, QK_PAD), lambda b, h, q: (b, h, 0, 0, 0)),
            pl.BlockSpec((1, nk, V_HEAD, TK), lambda b, h, q: (b, 0, h, 0)),
        ],
        out_specs=pl.BlockSpec((1, TQ, V_HEAD), lambda b, h, q: (b, q, h)),
        out_shape=jax.ShapeDtypeStruct((batch, seq, N_HEADS * V_HEAD), jnp.bfloat16),
        scratch_shapes=[
            pltpu.VMEM((TK, TQ), jnp.float32), pltpu.VMEM((TK, TQ), jnp.float32),
            pltpu.VMEM((TK, TQ), jnp.bfloat16), pltpu.VMEM((TK, TQ), jnp.bfloat16),
            pltpu.VMEM((V_HEAD, TQ), jnp.float32),
        ],
        compiler_params=pltpu.CompilerParams(
            dimension_semantics=("parallel", "parallel", "arbitrary"),
            vmem_limit_bytes=VMEM_LIMIT),
        name="attention",
    )(qt, k, vt)


def _mix_kernel(tiles_per_seq, h_ref, a_ref, cb_ref, p_ref, pprev_ref, pnext_ref, pre_ref,
                wga_ref, wgc_ref, bga_ref, bgc_ref, cw_ref, wco_ref, wo_ref, post_ref,
                o_ref, u_ref, cvin_ref, acc_ref):
    i = pl.program_id(0)
    j = pl.program_id(1)
    tm = h_ref.shape[0]

    @pl.when(j == 0)
    def _():
        u_ref[...] = _rms(h_ref[...], pre_ref[...]).astype(jnp.bfloat16)
        p = p_ref[...].astype(jnp.float32)
        s_idx = i % tiles_per_seq
        prev_row = jnp.where(s_idx == 0, 0.0, pprev_ref[7:8, :].astype(jnp.float32))
        next_row = jnp.where(s_idx == tiles_per_seq - 1, 0.0,
                             pnext_ref[0:1, :].astype(jnp.float32))
        row = lax.broadcasted_iota(jnp.int32, p.shape, 0)
        p_before = jnp.where(row == 0, prev_row, pltpu.roll(p, 1, 0))
        p_after = jnp.where(row == tm - 1, next_row, pltpu.roll(p, tm - 1, 0))
        conv = p_before * cw_ref[0:1, :] + p * cw_ref[1:2, :] + p_after * cw_ref[2:3, :]
        cvin_ref[...] = (cb_ref[...].astype(jnp.float32) * conv).astype(jnp.bfloat16)
        acc_ref[...] = jnp.zeros_like(acc_ref)

    u = u_ref[...]
    g_a = _sigmoid(_dot(u, wga_ref[...]) + bga_ref[...])
    g_c = _sigmoid(_dot(u, wgc_ref[...]) + bgc_ref[...])
    cv = _dot(cvin_ref[...], wco_ref[...])
    mixed = (g_a * a_ref[...].astype(jnp.float32) + g_c * cv).astype(jnp.bfloat16)
    acc_ref[...] += _dot(mixed, wo_ref[...])

    @pl.when(j == pl.num_programs(1) - 1)
    def _():
        o_ref[...] = h_ref[...] + _rms(acc_ref[...], post_ref[...])


def _mix_out(h, a, cb, p, seq, pre, w_g, b_g, conv_w, w_co, w_o, post):
    n, d = h.shape
    tm, tc = TM_MIX, TC_MIX
    nj = d // tc
    halo = 8
    per_tile = tm // halo
    last_halo = n // halo - 1
    return pl.pallas_call(
        functools.partial(_mix_kernel, seq // tm),
        grid=(n // tm, nj),
        in_specs=[
            pl.BlockSpec((tm, d), lambda i, j: (i, 0)),
            pl.BlockSpec((tm, tc), lambda i, j: (i, j)),
            pl.BlockSpec((tm, D_CONV), lambda i, j: (i, 0)),
            pl.BlockSpec((tm, D_CONV), lambda i, j: (i, 0)),
            pl.BlockSpec((halo, D_CONV), lambda i, j: (jnp.maximum(i * per_tile - 1, 0), 0)),
            pl.BlockSpec((halo, D_CONV), lambda i, j: (jnp.minimum((i + 1) * per_tile, last_halo), 0)),
            pl.BlockSpec((1, d), lambda i, j: (0, 0)),
            pl.BlockSpec((d, tc), lambda i, j: (0, j)),
            pl.BlockSpec((d, tc), lambda i, j: (0, j + nj)),
            pl.BlockSpec((1, tc), lambda i, j: (0, j)),
            pl.BlockSpec((1, tc), lambda i, j: (0, j + nj)),
            pl.BlockSpec((3, D_CONV), lambda i, j: (0, 0)),
            pl.BlockSpec((D_CONV, tc), lambda i, j: (0, j)),
            pl.BlockSpec((tc, d), lambda i, j: (j, 0)),
            pl.BlockSpec((1, d), lambda i, j: (0, 0)),
        ],
        out_specs=pl.BlockSpec((tm, d), lambda i, j: (i, 0)),
        out_shape=jax.ShapeDtypeStruct((n, d), jnp.float32),
        scratch_shapes=[
            pltpu.VMEM((tm, d), jnp.bfloat16),
            pltpu.VMEM((tm, D_CONV), jnp.bfloat16),
            pltpu.VMEM((tm, d), jnp.float32),
        ],
        compiler_params=pltpu.CompilerParams(
            dimension_semantics=("parallel", "arbitrary"), vmem_limit_bytes=VMEM_LIMIT),
        name="mix_out",
    )(h, a, cb, p, p, p, pre, w_g, w_g, b_g, b_g, conv_w, w_co, w_o, post)


def _rope_tables(seq):
    pos = jnp.arange(seq, dtype=jnp.float32)
    inv_freq = ROPE_THETA ** (-jnp.arange(0, QK_ROPE, 2, dtype=jnp.float32) / QK_ROPE)
    ang = pos[:, None] * inv_freq[None, :]
    cos, sin = jnp.cos(ang), jnp.sin(ang)
    zeros = jnp.zeros((seq, LANES - QK_ROPE), jnp.float32)
    kcos = jnp.concatenate([cos, cos, zeros], axis=1)
    ksin = jnp.concatenate([-sin, sin, zeros], axis=1)
    return kcos, ksin, cos.T, sin.T


def _prep_weights(ffn1_pre, ffn1_w_gu, ffn1_w_down, ffn1_post, mix_pre, w_in, b_gate, q_norm,
                  kv_norm, w_uq, w_ukv, conv_w, w_conv_out, w_o, mix_post, ffn2_pre, ffn2_w_gu,
                  ffn2_w_down, ffn2_post):
    bf = jnp.bfloat16
    d = w_in.shape[0]
    row = lambda v: v.reshape(1, -1)
    off_cb = KR_OFF + QK_ROPE
    off_g = off_cb + 3 * D_CONV
    w_lat = jnp.concatenate(
        [w_in[:, :off_cb], jnp.zeros((d, LAT_PAD - off_cb), w_in.dtype)], axis=1).astype(bf)
    w_ukv3 = w_ukv.reshape(KV_LORA, N_HEADS, QK_NOPE + V_HEAD)
    return dict(
        ffn1=(row(ffn1_pre), ffn1_w_gu.astype(bf), ffn1_w_down.astype(bf), row(ffn1_post)),
        ffn2=(row(ffn2_pre), ffn2_w_gu.astype(bf), ffn2_w_down.astype(bf), row(ffn2_post)),
        mix_pre=row(mix_pre),
        w_lat=w_lat,
        w_conv=w_in[:, off_cb:off_g].astype(bf),
        w_g=w_in[:, off_g:].astype(bf),
        b_g=row(b_gate),
        q_norm=row(q_norm),
        kv_norm=row(kv_norm),
        w_qt=w_uq.T.astype(bf),
        w_uk=w_ukv3[:, :, :QK_NOPE].reshape(KV_LORA, N_HEADS * QK_NOPE).astype(bf),
        w_vt=w_ukv3[:, :, QK_NOPE:].reshape(KV_LORA, N_HEADS * V_HEAD).T.astype(bf),
        conv_w=conv_w,
        w_co=w_conv_out.astype(bf),
        w_o=w_o.astype(bf),
        mix_post=row(mix_post),
    )


def _layer(x, w, tables):
    batch, seq, d = x.shape
    x2 = x.reshape(batch * seq, d)
    h = _ffn(x2, *w["ffn1"])
    lat, cb, p = _in_proj(h, w["mix_pre"], w["w_lat"], w["w_conv"])
    qt, k, vt = _qkv(lat, batch, seq, w["q_norm"], w["kv_norm"], w["w_qt"], w["w_uk"],
                     w["w_vt"], *tables)
    a = _attention(qt, k, vt).reshape(batch * seq, d)
    h2 = _mix_out(h, a, cb, p, seq, w["mix_pre"], w["w_g"], w["b_g"], w["conv_w"],
                  w["w_co"], w["w_o"], w["mix_post"])
    y = _ffn(h2, *w["ffn2"])
    return y.reshape(batch, seq, d)


def kernel(x_prompt, x_sample, ffn1_pre, ffn1_w_gu, ffn1_w_down, ffn1_post, mix_pre, w_in, b_gate, q_norm, kv_norm, w_uq, w_ukv, conv_w, w_conv_out, w_o, mix_post, ffn2_pre, ffn2_w_gu, ffn2_w_down, ffn2_post):
    params = (ffn1_pre, ffn1_w_gu, ffn1_w_down, ffn1_post, mix_pre, w_in, b_gate, q_norm,
              kv_norm, w_uq, w_ukv, conv_w, w_conv_out, w_o, mix_post, ffn2_pre, ffn2_w_gu,
              ffn2_w_down, ffn2_post)
    depth = ffn1_pre.shape[0]
    tables = _rope_tables(max(x_prompt.shape[1], x_sample.shape[1]))
    y_prompt, y_sample = x_prompt, x_sample
    for l in range(depth):
        w = _prep_weights(*(t[l] for t in params))
        y_prompt = _layer(y_prompt, w, tables)
        y_sample = _layer(y_sample, w, tables)
    return (y_prompt, y_sample)
```

```python
import functools

import jax
import jax.numpy as jnp
from jax import lax
from jax.experimental import pallas as pl
from jax.experimental.pallas import tpu as pltpu

N_HEADS = 16
QK_NOPE = 128
QK_ROPE = 64
V_HEAD = 128
Q_LORA = 768
KV_LORA = 512
ROPE_THETA = 10000.0
D_CONV = 1024
EPS = 1e-6

QK_PAD = 256
LAT_PAD = 1408
KR_OFF = Q_LORA + KV_LORA
ROPE_HALF = QK_ROPE // 2
LANES = 128
V_EXT = V_HEAD + 16
NEG_BIG = -2.0 ** 100
LOG2E = 1.4426950408889634

TM_FFN = 512
TF_FFN = 512
TM_IN = 512
TM_QKV = 256
TQ = 1024
TK = 512
ATTN_GROUP = 8
TM_MIX = 512
TC_MIX = 512
VMEM_LIMIT = 56 * 1024 * 1024

_NT = (((1,), (1,)), ((), ()))


def _rms(x, g):
    ms = jnp.mean(x * x, axis=-1, keepdims=True)
    return x * lax.rsqrt(ms + EPS) * g


def _sigmoid(x):
    return 1.0 / (1.0 + jnp.exp(-x))


def _dot(a, b):
    return jnp.dot(a, b, preferred_element_type=jnp.float32)


def _resident(shape):
    zeros = (0,) * len(shape)
    return pl.BlockSpec(shape, lambda *_: zeros, pipeline_mode=pl.Buffered(1))


def _ffn_kernel(x_ref, pre_ref, wg_ref, wu_ref, wd_ref, post_ref, o_ref, xn_ref, acc_ref):
    j = pl.program_id(1)

    @pl.when(j == 0)
    def _():
        xn_ref[...] = _rms(x_ref[...], pre_ref[...]).astype(jnp.bfloat16)
        acc_ref[...] = jnp.zeros_like(acc_ref)

    xn = xn_ref[...]
    gate = _dot(xn, wg_ref[...])
    up = _dot(xn, wu_ref[...])
    act = (gate * _sigmoid(gate) * up).astype(jnp.bfloat16)
    acc_ref[...] += _dot(act, wd_ref[...])

    @pl.when(j == pl.num_programs(1) - 1)
    def _():
        o_ref[...] = x_ref[...] + 0.5 * _rms(acc_ref[...], post_ref[...])


def _ffn(x, pre, w_gu, w_down, post):
    n, d = x.shape
    d_ff = w_down.shape[0]
    tm, tf = TM_FFN, TF_FFN
    nj = d_ff // tf
    return pl.pallas_call(
        _ffn_kernel,
        grid=(n // tm, nj),
        in_specs=[
            pl.BlockSpec((tm, d), lambda i, j: (i, 0)),
            pl.BlockSpec((1, d), lambda i, j: (0, 0)),
            pl.BlockSpec((d, tf), lambda i, j: (0, j)),
            pl.BlockSpec((d, tf), lambda i, j: (0, j + nj)),
            pl.BlockSpec((tf, d), lambda i, j: (j, 0)),
            pl.BlockSpec((1, d), lambda i, j: (0, 0)),
        ],
        out_specs=pl.BlockSpec((tm, d), lambda i, j: (i, 0)),
        out_shape=jax.ShapeDtypeStruct((n, d), jnp.float32),
        scratch_shapes=[pltpu.VMEM((tm, d), jnp.bfloat16), pltpu.VMEM((tm, d), jnp.float32)],
        compiler_params=pltpu.CompilerParams(
            dimension_semantics=("parallel", "arbitrary"), vmem_limit_bytes=VMEM_LIMIT),
        name="ffn",
    )(x, pre, w_gu, w_gu, w_down, post)


def _in_proj_kernel(h_ref, pre_ref, wlat_ref, wconv_ref, lat_ref, cb_ref, p_ref):
    u = _rms(h_ref[...], pre_ref[...]).astype(jnp.bfloat16)
    lat_ref[...] = _dot(u, wlat_ref[...])
    cb_ref[...] = _dot(u, wconv_ref[:, :D_CONV]).astype(jnp.bfloat16)
    cc = _dot(u, wconv_ref[:, D_CONV:2 * D_CONV])
    cx = _dot(u, wconv_ref[:, 2 * D_CONV:])
    p_ref[...] = (cc * cx).astype(jnp.bfloat16)


def _in_proj(h, pre, w_lat, w_conv):
    n, d = h.shape
    tm = TM_IN
    return pl.pallas_call(
        _in_proj_kernel,
        grid=(n // tm,),
        in_specs=[
            pl.BlockSpec((tm, d), lambda i: (i, 0)),
            _resident((1, d)),
            _resident(w_lat.shape),
            _resident(w_conv.shape),
        ],
        out_specs=[
            pl.BlockSpec((tm, LAT_PAD), lambda i: (i, 0)),
            pl.BlockSpec((tm, D_CONV), lambda i: (i, 0)),
            pl.BlockSpec((tm, D_CONV), lambda i: (i, 0)),
        ],
        out_shape=[
            jax.ShapeDtypeStruct((n, LAT_PAD), jnp.float32),
            jax.ShapeDtypeStruct((n, D_CONV), jnp.bfloat16),
            jax.ShapeDtypeStruct((n, D_CONV), jnp.bfloat16),
        ],
        compiler_params=pltpu.CompilerParams(
            dimension_semantics=("parallel",), vmem_limit_bytes=VMEM_LIMIT),
        name="in_proj",
    )(h, pre, w_lat, w_conv)


def _qkv_kernel(lat_ref, qn_ref, kvn_ref, wqt_ref, wuk_ref, wvt_ref, kc_ref, ks_ref,
                cost_ref, sint_ref, qt_ref, k_ref, vt_ref):
    tm = lat_ref.shape[0]
    qn = _rms(lat_ref[:, :Q_LORA], qn_ref[...]).astype(jnp.bfloat16)
    kvn = _rms(lat_ref[:, Q_LORA:KR_OFF], kvn_ref[...]).astype(jnp.bfloat16)

    kr = lat_ref[:, KR_OFF:]
    lane = lax.broadcasted_iota(jnp.int32, kr.shape, 1)
    swapped = jnp.where(lane < ROPE_HALF,
                        pltpu.roll(kr, LANES - ROPE_HALF, 1), pltpu.roll(kr, ROPE_HALF, 1))
    kro = (kr * kc_ref[...] + swapped * ks_ref[...]).astype(jnp.bfloat16)

    kn = _dot(kvn, wuk_ref[...]).astype(jnp.bfloat16)
    for h in range(N_HEADS):
        k_ref[0, h, :, :QK_NOPE] = kn[:, h * QK_NOPE:(h + 1) * QK_NOPE]
        k_ref[0, h, :, QK_NOPE:] = kro

    vt = lax.dot_general(wvt_ref[...], kvn, _NT, preferred_element_type=jnp.float32)
    vt_ref[0, 0, :, :V_HEAD, :] = vt.reshape(N_HEADS, V_HEAD, tm).astype(jnp.bfloat16)
    extra = lax.broadcasted_iota(jnp.int32, (N_HEADS, V_EXT - V_HEAD, tm), 1)
    vt_ref[0, 0, :, V_HEAD:, :] = jnp.where(extra == 0, 1.0, 0.0).astype(jnp.bfloat16)

    scale = (QK_NOPE + QK_ROPE) ** -0.5 * LOG2E
    qt =lax.dot_general(wqt_ref[...], qn, _NT, preferred_element_type=jnp.float32) * scale
    qt = qt.reshape(N_HEADS, QK_NOPE + QK_ROPE, tm)
    x1 = qt[:, QK_NOPE:QK_NOPE + ROPE_HALF, :]
    x2 = qt[:, QK_NOPE + ROPE_HALF:, :]
    cos = cost_ref[...][None]
    sin = sint_ref[...][None]
    qt_ref[0, :, :QK_NOPE, :] = qt[:, :QK_NOPE, :].astype(jnp.bfloat16)
    qt_ref[0, :, QK_NOPE:QK_NOPE + ROPE_HALF, :] = (x1 * cos - x2 * sin).astype(jnp.bfloat16)
    qt_ref[0, :, QK_NOPE + ROPE_HALF:QK_NOPE + QK_ROPE, :] = (x2 * cos + x1 * sin).astype(jnp.bfloat16)
    qt_ref[0, :, QK_NOPE + QK_ROPE:, :] = jnp.zeros(
        (N_HEADS, QK_PAD - QK_NOPE - QK_ROPE, tm), jnp.bfloat16)


def _qkv(lat, batch, seq, q_norm, kv_norm, w_qt, w_uk, w_vt, kcos, ksin, cos_t, sin_t):
    tm = TM_QKV
    ns = seq // tm
    per_chunk = TK // tm
    return pl.pallas_call(
        _qkv_kernel,
        grid=(batch, ns),
        in_specs=[
            pl.BlockSpec((tm, LAT_PAD), lambda b, s: (b * ns + s, 0)),
            _resident((1, Q_LORA)),
            _resident((1, KV_LORA)),
            _resident(w_qt.shape),
            _resident(w_uk.shape),
            _resident(w_vt.shape),
            pl.BlockSpec((tm, LANES), lambda b, s: (s, 0)),
            pl.BlockSpec((tm, LANES), lambda b, s: (s, 0)),
            pl.BlockSpec((ROPE_HALF, tm), lambda b, s: (0, s)),
            pl.BlockSpec((ROPE_HALF, tm), lambda b, s: (0, s)),
        ],
        out_specs=[
            pl.BlockSpec((1, N_HEADS, QK_PAD, tm), lambda b, s: (b, 0, 0, s)),
            pl.BlockSpec((1, N_HEADS, tm, QK_PAD), lambda b, s: (b, 0, s, 0)),
            pl.BlockSpec((1, 1, N_HEADS, V_EXT, tm),
                         lambda b, s: (b, s // per_chunk, 0, 0, s % per_chunk)),
        ],
        out_shape=[
            jax.ShapeDtypeStruct((batch, N_HEADS, QK_PAD, seq), jnp.bfloat16),
            jax.ShapeDtypeStruct((batch, N_HEADS, seq, QK_PAD), jnp.bfloat16),
            jax.ShapeDtypeStruct((batch, seq // TK, N_HEADS, V_EXT, TK), jnp.bfloat16),
        ],
        compiler_params=pltpu.CompilerParams(
            dimension_semantics=("parallel", "parallel"), vmem_limit_bytes=VMEM_LIMIT),
        name="qkv",
    )(lat, q_norm, kv_norm, w_qt, w_uk, w_vt, kcos, ksin, cos_t, sin_t)


def _attn_kernel(qt_ref, k_ref, vt_ref, o_ref, s0_ref, s1_ref, s2_ref, s3_ref, p0_ref, p1_ref,
                 acc_ref):
    nk = k_ref.shape[2]
    tq = qt_ref.shape[3]
    qt = qt_ref[0, 0]

    def scores(c, s_ref):
        s = _dot(k_ref[0, 0, c], qt).astype(jnp.bfloat16)
        s_ref[...] = s
        return jnp.max(s, axis=0, keepdims=True).astype(jnp.float32)

    def softmax(s_ref, p_ref, s_max, m_prev):
        m_new = jnp.maximum(m_prev, s_max)
        alpha = jnp.exp2(m_prev - m_new)
        p_ref[...] = jnp.exp2(s_ref[...].astype(jnp.float32) - m_new).astype(jnp.bfloat16)
        return m_new, alpha

    def apply(c, p_ref, alpha):
        acc_ref[...] = alpha * acc_ref[...] + _dot(vt_ref[0, c, 0], p_ref[...])

    s_refs = (s0_ref, s1_ref, s2_ref, s3_ref)
    p_refs = (p0_ref, p1_ref)
    ahead = 2
    group_len = ATTN_GROUP

    def group(i, carry, first, last):
        m, alpha, s_max, s_max_1 = carry
        for t in range(group_len):
            c = group_len * i + t
            s_max_2 = s_max_1
            if not (last and t + ahead >= group_len):
                s_max_2 = scores(c + ahead, s_refs[(t + ahead) % 4])
            if not (first and t == 0):
                apply(c - 1, p_refs[(t - 1) % 2], alpha)
            m, alpha = softmax(s_refs[t % 4], p_refs[t % 2], s_max, m)
            s_max, s_max_1 = s_max_1, s_max_2
        return m, alpha, s_max, s_max_1

    acc_ref[...] = jnp.zeros_like(acc_ref)
    s_max_0 = scores(0, s0_ref)
    s_max_1 = scores(1, s1_ref)
    carry = (jnp.full((1, tq), NEG_BIG, jnp.float32), jnp.ones((1, tq), jnp.float32),
             s_max_0, s_max_1)
    n_groups = nk // group_len
    if n_groups == 1:
        carry = group(0, carry, True, True)
    else:
        carry = group(0, carry, True, False)
        carry = lax.fori_loop(1, n_groups - 1, lambda i, c: group(i, c, False, False), carry)
        carry = group(n_groups - 1, carry, False, True)
    alpha = carry[1]
    apply(nk - 1, p_refs[(nk - 1) % 2], alpha)
    o_ref[0] = (acc_ref[:V_HEAD, :] / acc_ref[V_HEAD:V_HEAD + 1, :]).T.astype(jnp.bfloat16)


def _attention(qt, k, vt):
    batch, _, _, seq = qt.shape
    nk = seq // TK
    assert seq % TQ == 0 and nk % ATTN_GROUP == 0 and ATTN_GROUP % 4 == 0, (seq, TQ, TK)
    k = k.reshape(batch, N_HEADS, nk, TK, QK_PAD)
    return pl.pallas_call(
        _attn_kernel,
        grid=(batch, N_HEADS, seq // TQ),
        in_specs=[
            pl.BlockSpec((1, 1, QK_PAD, TQ), lambda b, h, q: (b, h, 0, q)),
            pl.BlockSpec((1, 1, nk, TK, QK_PAD), lambda b, h, q: (b, h, 0, 0, 0)),
            pl.BlockSpec((1, nk, 1, V_EXT, TK), lambda b, h, q: (b, 0, h, 0, 0)),
        ],
        out_specs=pl.BlockSpec((1, TQ, V_HEAD), lambda b, h, q: (b, q, h)),
        out_shape=jax.ShapeDtypeStruct((batch, seq, N_HEADS * V_HEAD), jnp.bfloat16),
        scratch_shapes=[
            *[pltpu.VMEM((TK, TQ), jnp.bfloat16) for _ in range(6)],
            pltpu.VMEM((V_EXT, TQ), jnp.float32),
        ],
        compiler_params=pltpu.CompilerParams(
            dimension_semantics=("parallel", "parallel", "arbitrary"),
            vmem_limit_bytes=VMEM_LIMIT),
        name="attention",
    )(qt, k, vt)


def _mix_kernel(tiles_per_seq, h_ref, a_ref, cb_ref, p_ref, pprev_ref, pnext_ref, pre_ref,
                wga_ref, wgc_ref, bga_ref, bgc_ref, cw_ref, wco_ref, wo_ref, post_ref,
                o_ref, u_ref, cvin_ref, acc_ref):
    i = pl.program_id(0)
    j = pl.program_id(1)
    tm = h_ref.shape[0]

    @pl.when(j == 0)
    def _():
        u_ref[...] = _rms(h_ref[...], pre_ref[...]).astype(jnp.bfloat16)
        p = p_ref[...].astype(jnp.float32)
        s_idx = i % tiles_per_seq
        prev_row = jnp.where(s_idx == 0, 0.0, pprev_ref[7:8, :].astype(jnp.float32))
        next_row = jnp.where(s_idx == tiles_per_seq - 1, 0.0,
                             pnext_ref[0:1, :].astype(jnp.float32))
        row = lax.broadcasted_iota(jnp.int32, p.shape, 0)
        p_before = jnp.where(row == 0, prev_row, pltpu.roll(p, 1, 0))
        p_after = jnp.where(row == tm - 1, next_row, pltpu.roll(p, tm - 1, 0))
        conv = p_before * cw_ref[0:1, :] + p * cw_ref[1:2, :] + p_after * cw_ref[2:3, :]
        cvin_ref[...] = (cb_ref[...].astype(jnp.float32) * conv).astype(jnp.bfloat16)
        acc_ref[...] = jnp.zeros_like(acc_ref)

    u = u_ref[...]
    g_a = _sigmoid(_dot(u, wga_ref[...]) + bga_ref[...])
    g_c = _sigmoid(_dot(u, wgc_ref[...]) + bgc_ref[...])
    cv = _dot(cvin_ref[...], wco_ref[...])
    mixed = (g_a * a_ref[...].astype(jnp.float32) + g_c * cv).astype(jnp.bfloat16)
    acc_ref[...] += _dot(mixed, wo_ref[...])

    @pl.when(j == pl.num_programs(1) - 1)
    def _():
        o_ref[...] = h_ref[...] + _rms(acc_ref[...], post_ref[...])


def _mix_out(h, a, cb, p, seq, pre, w_g, b_g, conv_w, w_co, w_o, post):
    n, d = h.shape
    tm, tc = TM_MIX, TC_MIX
    nj = d // tc
    halo = 8
    per_tile = tm // halo
    last_halo = n // halo - 1
    return pl.pallas_call(
        functools.partial(_mix_kernel, seq // tm),
        grid=(n // tm, nj),
        in_specs=[
            pl.BlockSpec((tm, d), lambda i, j: (i, 0)),
            pl.BlockSpec((tm, tc), lambda i, j: (i, j)),
            pl.BlockSpec((tm, D_CONV), lambda i, j: (i, 0)),
            pl.BlockSpec((tm, D_CONV), lambda i, j: (i, 0)),
            pl.BlockSpec((halo, D_CONV), lambda i, j: (jnp.maximum(i * per_tile - 1, 0), 0)),
            pl.BlockSpec((halo, D_CONV), lambda i, j: (jnp.minimum((i + 1) * per_tile, last_halo), 0)),
            pl.BlockSpec((1, d), lambda i, j: (0, 0)),
            pl.BlockSpec((d, tc), lambda i, j: (0, j)),
            pl.BlockSpec((d, tc), lambda i, j: (0, j + nj)),
            pl.BlockSpec((1, tc), lambda i, j: (0, j)),
            pl.BlockSpec((1, tc), lambda i, j: (0, j + nj)),
            pl.BlockSpec((3, D_CONV), lambda i, j: (0, 0)),
            pl.BlockSpec((D_CONV, tc), lambda i, j: (0, j)),
            pl.BlockSpec((tc, d), lambda i, j: (j, 0)),
            pl.BlockSpec((1, d), lambda i, j: (0, 0)),
        ],
        out_specs=pl.BlockSpec((tm, d), lambda i, j: (i, 0)),
        out_shape=jax.ShapeDtypeStruct((n, d), jnp.float32),
        scratch_shapes=[
            pltpu.VMEM((tm, d), jnp.bfloat16),
            pltpu.VMEM((tm, D_CONV), jnp.bfloat16),
            pltpu.VMEM((tm, d), jnp.float32),
        ],
        compiler_params=pltpu.CompilerParams(
            dimension_semantics=("parallel", "arbitrary"), vmem_limit_bytes=VMEM_LIMIT),
        name="mix_out",
    )(h, a, cb, p, p, p, pre, w_g, w_g, b_g, b_g, conv_w, w_co, w_o, post)


def _rope_tables(seq):
    pos = jnp.arange(seq, dtype=jnp.float32)
    inv_freq = ROPE_THETA ** (-jnp.arange(0, QK_ROPE, 2, dtype=jnp.float32) / QK_ROPE)
    ang = pos[:, None] * inv_freq[None, :]
    cos, sin = jnp.cos(ang), jnp.sin(ang)
    zeros = jnp.zeros((seq, LANES - QK_ROPE), jnp.float32)
    kcos = jnp.concatenate([cos, cos, zeros], axis=1)
    ksin = jnp.concatenate([-sin, sin, zeros], axis=1)
    return kcos, ksin, cos.T, sin.T


def _prep_weights(ffn1_pre, ffn1_w_gu, ffn1_w_down, ffn1_post, mix_pre, w_in, b_gate, q_norm,
                  kv_norm, w_uq, w_ukv, conv_w, w_conv_out, w_o, mix_post, ffn2_pre, ffn2_w_gu,
                  ffn2_w_down, ffn2_post):
    bf = jnp.bfloat16
    d = w_in.shape[0]
    row = lambda v: v.reshape(1, -1)
    off_cb = KR_OFF + QK_ROPE
    off_g = off_cb + 3 * D_CONV
    w_lat = jnp.concatenate(
        [w_in[:, :off_cb], jnp.zeros((d, LAT_PAD - off_cb), w_in.dtype)], axis=1).astype(bf)
    w_ukv3 = w_ukv.reshape(KV_LORA, N_HEADS, QK_NOPE + V_HEAD)
    return dict(
        ffn1=(row(ffn1_pre), ffn1_w_gu.astype(bf), ffn1_w_down.astype(bf), row(ffn1_post)),
        ffn2=(row(ffn2_pre), ffn2_w_gu.astype(bf), ffn2_w_down.astype(bf), row(ffn2_post)),
        mix_pre=row(mix_pre),
        w_lat=w_lat,
        w_conv=w_in[:, off_cb:off_g].astype(bf),
        w_g=w_in[:, off_g:].astype(bf),
        b_g=row(b_gate),
        q_norm=row(q_norm),
        kv_norm=row(kv_norm),
        w_qt=w_uq.T.astype(bf),
        w_uk=w_ukv3[:, :, :QK_NOPE].reshape(KV_LORA, N_HEADS * QK_NOPE).astype(bf),
        w_vt=w_ukv3[:, :, QK_NOPE:].reshape(KV_LORA, N_HEADS * V_HEAD).T.astype(bf),
        conv_w=conv_w,
        w_co=w_conv_out.astype(bf),
        w_o=w_o.astype(bf),
        mix_post=row(mix_post),
    )


def _layer(x, w, tables):
    batch, seq, d = x.shape
    x2 = x.reshape(batch * seq, d)
    h = _ffn(x2, *w["ffn1"])
    lat, cb, p = _in_proj(h, w["mix_pre"], w["w_lat"], w["w_conv"])
    qt, k, vt = _qkv(lat, batch, seq, w["q_norm"], w["kv_norm"], w["w_qt"], w["w_uk"],
                     w["w_vt"], *tables)
    a = _attention(qt, k, vt).reshape(batch * seq, d)
    h2 = _mix_out(h, a, cb, p, seq, w["mix_pre"], w["w_g"], w["b_g"], w["conv_w"],
                  w["w_co"], w["w_o"], w["mix_post"])
    y = _ffn(h2, *w["ffn2"])
    return y.reshape(batch, seq, d)


def kernel(x_prompt, x_sample, ffn1_pre, ffn1_w_gu, ffn1_w_down, ffn1_post, mix_pre, w_in, b_gate, q_norm, kv_norm, w_uq, w_ukv, conv_w, w_conv_out, w_o, mix_post, ffn2_pre, ffn2_w_gu, ffn2_w_down, ffn2_post):
    params = (ffn1_pre, ffn1_w_gu, ffn1_w_down, ffn1_post, mix_pre, w_in, b_gate, q_norm,
              kv_norm, w_uq, w_ukv, conv_w, w_conv_out, w_o, mix_post, ffn2_pre, ffn2_w_gu,
              ffn2_w_down, ffn2_post)
    depth = ffn1_pre.shape[0]
    tables = _rope_tables(max(x_prompt.shape[1], x_sample.shape[1]))
    y_prompt, y_sample = x_prompt, x_sample
    for l in range(depth):
        w = _prep_weights(*(t[l] for t in params))
        y_prompt = _layer(y_prompt, w, tables)
        y_sample = _layer(y_sample, w, tables)
    return (y_prompt, y_sample)
```

```python
import functools

import jax
import jax.numpy as jnp
from jax import lax
from jax.experimental import pallas as pl
from jax.experimental.pallas import tpu as pltpu

N_HEADS = 16
QK_NOPE = 128
QK_ROPE = 64
V_HEAD = 128
Q_LORA = 768
KV_LORA = 512
ROPE_THETA = 10000.0
D_CONV = 1024
EPS = 1e-6

QK_PAD = 256
LAT_PAD = 1408
KR_OFF = Q_LORA + KV_LORA
ROPE_HALF = QK_ROPE // 2
LANES = 128
V_EXT = V_HEAD + 16
NEG_BIG = -2.0 ** 100
LOG2E = 1.4426950408889634

TM_FFN = 512
TF_FFN = 512
TM_IN = 512
TM_QKV = 256
TQ = 1024
TK = 512
ATTN_GROUP = 8
TM_MIX = 512
TC_MIX = 512
VMEM_LIMIT = 56 * 1024 * 1024

_NT = (((1,), (1,)), ((), ()))


def _rms(x, g):
    ms = jnp.mean(x * x, axis=-1, keepdims=True)
    return x * lax.rsqrt(ms + EPS) * g


def _sigmoid(x):
    return 1.0 / (1.0 + jnp.exp(-x))


def _dot(a, b):
    return jnp.dot(a, b, preferred_element_type=jnp.float32)


def _resident(shape):
    zeros = (0,) * len(shape)
    return pl.BlockSpec(shape, lambda *_: zeros, pipeline_mode=pl.Buffered(1))


def _ffn_kernel(x_ref, pre_ref, wg_ref, wu_ref, wd_ref, post_ref, o_ref, xn_ref, acc_ref):
    j = pl.program_id(1)
    last = pl.num_programs(1) - 1
    tm = x_ref.shape[0]
    halves = [pl.ds(r * (tm // 2), tm // 2) for r in range(2)]

    def chunk(xn):
        gate = _dot(xn, wg_ref[...])
        up = _dot(xn, wu_ref[...])
        act = (gate * _sigmoid(gate) * up).astype(jnp.bfloat16)
        return _dot(act, wd_ref[...])

    @pl.when(j == 0)
    def _():
        for rows in halves:
            xn = _rms(x_ref[rows, :], pre_ref[...]).astype(jnp.bfloat16)
            xn_ref[rows, :] = xn
            acc_ref[rows, :] = chunk(xn)

    @pl.when((j > 0) & (j < last))
    def _():
        acc_ref[...] += chunk(xn_ref[...])

    @pl.when(j == last)
    def _():
        for rows in halves:
            y = acc_ref[rows, :] + chunk(xn_ref[rows, :])
            o_ref[rows, :] = x_ref[rows, :] + 0.5 * _rms(y, post_ref[...])


def _ffn(x, pre, w_gu, w_down, post):
    n, d = x.shape
    d_ff = w_down.shape[0]
    tm, tf = TM_FFN, TF_FFN
    nj = d_ff // tf
    assert nj >= 2, nj
    return pl.pallas_call(
        _ffn_kernel,
        grid=(n // tm, nj),
        in_specs=[
            pl.BlockSpec((tm, d), lambda i, j: (i, 0)),
            pl.BlockSpec((1, d), lambda i, j: (0, 0)),
            pl.BlockSpec((d, tf), lambda i, j: (0, j)),
            pl.BlockSpec((d, tf), lambda i, j: (0, j + nj)),
            pl.BlockSpec((tf, d), lambda i, j: (j, 0)),
            pl.BlockSpec((1, d), lambda i, j: (0, 0)),
        ],
        out_specs=pl.BlockSpec((tm, d), lambda i, j: (i, 0)),
        out_shape=jax.ShapeDtypeStruct((n, d), jnp.float32),
        scratch_shapes=[pltpu.VMEM((tm, d), jnp.bfloat16), pltpu.VMEM((tm, d), jnp.float32)],
        compiler_params=pltpu.CompilerParams(
            dimension_semantics=("parallel", "arbitrary"), vmem_limit_bytes=VMEM_LIMIT),
        name="ffn",
    )(x, pre, w_gu, w_gu, w_down, post)


def _in_proj_kernel(h_ref, pre_ref, wlat_ref, wconv_ref, lat_ref, cb_ref, p_ref):
    u = _rms(h_ref[...], pre_ref[...]).astype(jnp.bfloat16)
    lat_ref[...] = _dot(u, wlat_ref[...])
    cb_ref[...] = _dot(u, wconv_ref[:, :D_CONV]).astype(jnp.bfloat16)
    cc = _dot(u, wconv_ref[:, D_CONV:2 * D_CONV])
    cx = _dot(u, wconv_ref[:, 2 * D_CONV:])
    p_ref[...] = (cc * cx).astype(jnp.bfloat16)


def _in_proj(h, pre, w_lat, w_conv):
    n, d = h.shape
    tm = TM_IN
    return pl.pallas_call(
        _in_proj_kernel,
        grid=(n // tm,),
        in_specs=[
            pl.BlockSpec((tm, d), lambda i: (i, 0)),
            _resident((1, d)),
            _resident(w_lat.shape),
            _resident(w_conv.shape),
        ],
        out_specs=[
            pl.BlockSpec((tm, LAT_PAD), lambda i: (i, 0)),
            pl.BlockSpec((tm, D_CONV), lambda i: (i, 0)),
            pl.BlockSpec((tm, D_CONV), lambda i: (i, 0)),
        ],
        out_shape=[
            jax.ShapeDtypeStruct((n, LAT_PAD), jnp.float32),
            jax.ShapeDtypeStruct((n, D_CONV), jnp.bfloat16),
            jax.ShapeDtypeStruct((n, D_CONV), jnp.bfloat16),
        ],
        compiler_params=pltpu.CompilerParams(
            dimension_semantics=("parallel",), vmem_limit_bytes=VMEM_LIMIT),
        name="in_proj",
    )(h, pre, w_lat, w_conv)


def _qkv_kernel(lat_ref, qn_ref, kvn_ref, wqt_ref, wuk_ref, wvt_ref, kc_ref, ks_ref,
                cost_ref, sint_ref, qt_ref, k_ref, vt_ref):
    tm = lat_ref.shape[0]
    qn = _rms(lat_ref[:, :Q_LORA], qn_ref[...]).astype(jnp.bfloat16)
    kvn = _rms(lat_ref[:, Q_LORA:KR_OFF], kvn_ref[...]).astype(jnp.bfloat16)

    kr = lat_ref[:, KR_OFF:]
    lane = lax.broadcasted_iota(jnp.int32, kr.shape, 1)
    swapped = jnp.where(lane < ROPE_HALF,
                        pltpu.roll(kr, LANES - ROPE_HALF, 1), pltpu.roll(kr, ROPE_HALF, 1))
    kro = (kr * kc_ref[...] + swapped * ks_ref[...]).astype(jnp.bfloat16)

    kn = _dot(kvn, wuk_ref[...]).astype(jnp.bfloat16)
    for h in range(N_HEADS):
        k_ref[0, h, :, :QK_NOPE] = kn[:, h * QK_NOPE:(h + 1) * QK_NOPE]
        k_ref[0, h, :, QK_NOPE:] = kro

    vt = lax.dot_general(wvt_ref[...], kvn, _NT, preferred_element_type=jnp.float32)
    vt_ref[0, 0, :, :V_HEAD, :] = vt.reshape(N_HEADS, V_HEAD, tm).astype(jnp.bfloat16)
    extra = lax.broadcasted_iota(jnp.int32, (N_HEADS, V_EXT - V_HEAD, tm), 1)
    vt_ref[0, 0, :, V_HEAD:, :] = jnp.where(extra == 0, 1.0, 0.0).astype(jnp.bfloat16)

    scale = (QK_NOPE + QK_ROPE) ** -0.5 * LOG2E
    qt =lax.dot_general(wqt_ref[...], qn, _NT, preferred_element_type=jnp.float32) * scale
    qt = qt.reshape(N_HEADS, QK_NOPE + QK_ROPE, tm)
    x1 = qt[:, QK_NOPE:QK_NOPE + ROPE_HALF, :]
    x2 = qt[:, QK_NOPE + ROPE_HALF:, :]
    cos = cost_ref[...][None]
    sin = sint_ref[...][None]
    qt_ref[0, :, :QK_NOPE, :] = qt[:, :QK_NOPE, :].astype(jnp.bfloat16)
    qt_ref[0, :, QK_NOPE:QK_NOPE + ROPE_HALF, :] = (x1 * cos - x2 * sin).astype(jnp.bfloat16)
    qt_ref[0, :, QK_NOPE + ROPE_HALF:QK_NOPE + QK_ROPE, :] = (x2 * cos + x1 * sin).astype(jnp.bfloat16)
    qt_ref[0, :, QK_NOPE + QK_ROPE:, :] = jnp.zeros(
        (N_HEADS, QK_PAD - QK_NOPE - QK_ROPE, tm), jnp.bfloat16)


def _qkv(lat, batch, seq, q_norm, kv_norm, w_qt, w_uk, w_vt, kcos, ksin, cos_t, sin_t):
    tm = TM_QKV
    ns = seq // tm
    per_chunk = TK // tm
    return pl.pallas_call(
        _qkv_kernel,
        grid=(batch, ns),
        in_specs=[
            pl.BlockSpec((tm, LAT_PAD), lambda b, s: (b * ns + s, 0)),
            _resident((1, Q_LORA)),
            _resident((1, KV_LORA)),
            _resident(w_qt.shape),
            _resident(w_uk.shape),
            _resident(w_vt.shape),
            pl.BlockSpec((tm, LANES), lambda b, s: (s, 0)),
            pl.BlockSpec((tm, LANES), lambda b, s: (s, 0)),
            pl.BlockSpec((ROPE_HALF, tm), lambda b, s: (0, s)),
            pl.BlockSpec((ROPE_HALF, tm), lambda b, s: (0, s)),
        ],
        out_specs=[
            pl.BlockSpec((1, N_HEADS, QK_PAD, tm), lambda b, s: (b, 0, 0, s)),
            pl.BlockSpec((1, N_HEADS, tm, QK_PAD), lambda b, s: (b, 0, s, 0)),
            pl.BlockSpec((1, 1, N_HEADS, V_EXT, tm),
                         lambda b, s: (b, s // per_chunk, 0, 0, s % per_chunk)),
        ],
        out_shape=[
            jax.ShapeDtypeStruct((batch, N_HEADS, QK_PAD, seq), jnp.bfloat16),
            jax.ShapeDtypeStruct((batch, N_HEADS, seq, QK_PAD), jnp.bfloat16),
            jax.ShapeDtypeStruct((batch, seq // TK, N_HEADS, V_EXT, TK), jnp.bfloat16),
        ],
        compiler_params=pltpu.CompilerParams(
            dimension_semantics=("parallel", "parallel"), vmem_limit_bytes=VMEM_LIMIT),
        name="qkv",
    )(lat, q_norm, kv_norm, w_qt, w_uk, w_vt, kcos, ksin, cos_t, sin_t)


def _attn_kernel(qt_ref, k_ref, vt_ref, o_ref, s0_ref, s1_ref, s2_ref, s3_ref, p0_ref, p1_ref,
                 acc_ref):
    nk = k_ref.shape[2]
    tq = qt_ref.shape[3]
    qt = qt_ref[0, 0]

    def scores(c, s_ref):
        s = _dot(k_ref[0, 0, c], qt).astype(jnp.bfloat16)
        s_ref[...] = s
        return jnp.max(s, axis=0, keepdims=True).astype(jnp.float32)

    def softmax(s_ref, p_ref, s_max, m_prev):
        m_new = jnp.maximum(m_prev, s_max)
        alpha = jnp.exp2(m_prev - m_new)
        p_ref[...] = jnp.exp2(s_ref[...].astype(jnp.float32) - m_new).astype(jnp.bfloat16)
        return m_new, alpha

    def apply(c, p_ref, alpha):
        acc_ref[...] = alpha * acc_ref[...] + _dot(vt_ref[0, c, 0], p_ref[...])

    s_refs = (s0_ref, s1_ref, s2_ref, s3_ref)
    p_refs = (p0_ref, p1_ref)
    ahead = 2
    group_len = ATTN_GROUP

    def group(i, carry, first, last):
        m, alpha, s_max, s_max_1 = carry
        for t in range(group_len):
            c = group_len * i + t
            s_max_2 = s_max_1
            if not (last and t + ahead >= group_len):
                s_max_2 = scores(c + ahead, s_refs[(t + ahead) % 4])
            if not (first and t == 0):
                apply(c - 1, p_refs[(t - 1) % 2], alpha)
            m, alpha = softmax(s_refs[t % 4], p_refs[t % 2], s_max, m)
            s_max, s_max_1 = s_max_1, s_max_2
        return m, alpha, s_max, s_max_1

    acc_ref[...] = jnp.zeros_like(acc_ref)
    s_max_0 = scores(0, s0_ref)
    s_max_1 = scores(1, s1_ref)
    carry = (jnp.full((1, tq), NEG_BIG, jnp.float32), jnp.ones((1, tq), jnp.float32),
             s_max_0, s_max_1)
    n_groups = nk // group_len
    if n_groups == 1:
        carry = group(0, carry, True, True)
    else:
        carry = group(0, carry, True, False)
        carry = lax.fori_loop(1, n_groups - 1, lambda i, c: group(i, c, False, False), carry)
        carry = group(n_groups - 1, carry, False, True)
    alpha = carry[1]
    apply(nk - 1, p_refs[(nk - 1) % 2], alpha)
    o_ref[0] = (acc_ref[:V_HEAD, :] / acc_ref[V_HEAD:V_HEAD + 1, :]).T.astype(jnp.bfloat16)


def _attention(qt, k, vt):
    batch, _, _, seq = qt.shape
    nk = seq // TK
    assert seq % TQ == 0 and nk % ATTN_GROUP == 0 and ATTN_GROUP % 4 == 0, (seq, TQ, TK)
    k = k.reshape(batch, N_HEADS, nk, TK, QK_PAD)
    return pl.pallas_call(
        _attn_kernel,
        grid=(batch, N_HEADS, seq // TQ),
        in_specs=[
            pl.BlockSpec((1, 1, QK_PAD, TQ), lambda b, h, q: (b, h, 0, q)),
            pl.BlockSpec((1, 1, nk, TK, QK_PAD), lambda b, h, q: (b, h, 0, 0, 0)),
            pl.BlockSpec((1, nk, 1, V_EXT, TK), lambda b, h, q: (b, 0, h, 0, 0)),
        ],
        out_specs=pl.BlockSpec((1, TQ, V_HEAD), lambda b, h, q: (b, q, h)),
        out_shape=jax.ShapeDtypeStruct((batch, seq, N_HEADS * V_HEAD), jnp.bfloat16),
        scratch_shapes=[
            *[pltpu.VMEM((TK, TQ), jnp.bfloat16) for _ in range(6)],
            pltpu.VMEM((V_EXT, TQ), jnp.float32),
        ],
        compiler_params=pltpu.CompilerParams(
            dimension_semantics=("parallel", "parallel", "arbitrary"),
            vmem_limit_bytes=VMEM_LIMIT),
        name="attention",
    )(qt, k, vt)


def _mix_kernel(tiles_per_seq, h_ref, a_ref, cb_ref, p_ref, pprev_ref, pnext_ref, pre_ref,
                wga_ref, wgc_ref, bga_ref, bgc_ref, cw_ref, wco_ref, wo_ref, post_ref,
                o_ref, u_ref, cvin_ref, acc_ref):
    i = pl.program_id(0)
    j = pl.program_id(1)
    last = pl.num_programs(1) - 1
    tm = h_ref.shape[0]
    hm = tm // 2
    halves = [pl.ds(r * hm, hm) for r in range(2)]

    def prepare(r):
        rows = halves[r]
        u_ref[rows, :] = _rms(h_ref[rows, :], pre_ref[...]).astype(jnp.bfloat16)
        p = p_ref[rows, :].astype(jnp.float32)
        s_idx = i % tiles_per_seq
        if r == 0:
            prev_row = jnp.where(s_idx == 0, 0.0, pprev_ref[7:8, :].astype(jnp.float32))
            next_row = p_ref[hm:hm + 1, :].astype(jnp.float32)
        else:
            prev_row = p_ref[hm - 1:hm, :].astype(jnp.float32)
            next_row = jnp.where(s_idx == tiles_per_seq - 1, 0.0,
                                 pnext_ref[0:1, :].astype(jnp.float32))
        row = lax.broadcasted_iota(jnp.int32, p.shape, 0)
        p_before = jnp.where(row == 0, prev_row, pltpu.roll(p, 1, 0))
        p_after = jnp.where(row == hm - 1, next_row, pltpu.roll(p, hm - 1, 0))
        conv = p_before * cw_ref[0:1, :] + p * cw_ref[1:2, :] + p_after * cw_ref[2:3, :]
        cvin_ref[rows, :] = (cb_ref[rows, :].astype(jnp.float32) * conv).astype(jnp.bfloat16)

    def project(rows=slice(None)):
        u = u_ref[rows, :]
        g_a = _sigmoid(_dot(u, wga_ref[...]) + bga_ref[...])
        g_c = _sigmoid(_dot(u, wgc_ref[...]) + bgc_ref[...])
        cv = _dot(cvin_ref[rows, :], wco_ref[...])
        mixed = (g_a * a_ref[rows, :].astype(jnp.float32) + g_c * cv).astype(jnp.bfloat16)
        return _dot(mixed, wo_ref[...])

    @pl.when(j == 0)
    def _():
        for r, rows in enumerate(halves):
            prepare(r)
            acc_ref[rows, :] = project(rows)

    @pl.when((j > 0) & (j < last))
    def _():
        acc_ref[...] += project()

    @pl.when(j == last)
    def _():
        for rows in halves:
            y = acc_ref[rows, :] + project(rows)
            o_ref[rows, :] = h_ref[rows, :] + _rms(y, post_ref[...])


def _mix_out(h, a, cb, p, seq, pre, w_g, b_g, conv_w, w_co, w_o, post):
    n, d = h.shape
    tm, tc = TM_MIX, TC_MIX
    nj = d // tc
    assert nj >= 2, nj
    halo = 8
    per_tile = tm // halo
    last_halo = n // halo - 1
    return pl.pallas_call(
        functools.partial(_mix_kernel, seq // tm),
        grid=(n // tm, nj),
        in_specs=[
            pl.BlockSpec((tm, d), lambda i, j: (i, 0)),
            pl.BlockSpec((tm, tc), lambda i, j: (i, j)),
            pl.BlockSpec((tm, D_CONV), lambda i, j: (i, 0)),
            pl.BlockSpec((tm, D_CONV), lambda i, j: (i, 0)),
            pl.BlockSpec((halo, D_CONV), lambda i, j: (jnp.maximum(i * per_tile - 1, 0), 0)),
            pl.BlockSpec((halo, D_CONV), lambda i, j: (jnp.minimum((i + 1) * per_tile, last_halo), 0)),
            pl.BlockSpec((1, d), lambda i, j: (0, 0)),
            pl.BlockSpec((d, tc), lambda i, j: (0, j)),
            pl.BlockSpec((d, tc), lambda i, j: (0, j + nj)),
            pl.BlockSpec((1, tc), lambda i, j: (0, j)),
            pl.BlockSpec((1, tc), lambda i, j: (0, j + nj)),
            pl.BlockSpec((3, D_CONV), lambda i, j: (0, 0)),
            pl.BlockSpec((D_CONV, tc), lambda i, j: (0, j)),
            pl.BlockSpec((tc, d), lambda i, j: (j, 0)),
            pl.BlockSpec((1, d), lambda i, j: (0, 0)),
        ],
        out_specs=pl.BlockSpec((tm, d), lambda i, j: (i, 0)),
        out_shape=jax.ShapeDtypeStruct((n, d), jnp.float32),
        scratch_shapes=[
            pltpu.VMEM((tm, d), jnp.bfloat16),
            pltpu.VMEM((tm, D_CONV), jnp.bfloat16),
            pltpu.VMEM((tm, d), jnp.float32),
        ],
        compiler_params=pltpu.CompilerParams(
            dimension_semantics=("parallel", "arbitrary"), vmem_limit_bytes=VMEM_LIMIT),
        name="mix_out",
    )(h, a, cb, p, p, p, pre, w_g, w_g, b_g, b_g, conv_w, w_co, w_o, post)


def _rope_tables(seq):
    pos = jnp.arange(seq, dtype=jnp.float32)
    inv_freq = ROPE_THETA ** (-jnp.arange(0, QK_ROPE, 2, dtype=jnp.float32) / QK_ROPE)
    ang = pos[:, None] * inv_freq[None, :]
    cos, sin = jnp.cos(ang), jnp.sin(ang)
    zeros = jnp.zeros((seq, LANES - QK_ROPE), jnp.float32)
    kcos = jnp.concatenate([cos, cos, zeros], axis=1)
    ksin = jnp.concatenate([-sin, sin, zeros], axis=1)
    return kcos, ksin, cos.T, sin.T


def _prep_weights(ffn1_pre, ffn1_w_gu, ffn1_w_down, ffn1_post, mix_pre, w_in, b_gate, q_norm,
                  kv_norm, w_uq, w_ukv, conv_w, w_conv_out, w_o, mix_post, ffn2_pre, ffn2_w_gu,
                  ffn2_w_down, ffn2_post):
    bf = jnp.bfloat16
    d = w_in.shape[0]
    row = lambda v: v.reshape(1, -1)
    off_cb = KR_OFF + QK_ROPE
    off_g = off_cb + 3 * D_CONV
    w_lat = jnp.concatenate(
        [w_in[:, :off_cb], jnp.zeros((d, LAT_PAD - off_cb), w_in.dtype)], axis=1).astype(bf)
    w_ukv3 = w_ukv.reshape(KV_LORA, N_HEADS, QK_NOPE + V_HEAD)
    return dict(
        ffn1=(row(ffn1_pre), ffn1_w_gu.astype(bf), ffn1_w_down.astype(bf), row(ffn1_post)),
        ffn2=(row(ffn2_pre), ffn2_w_gu.astype(bf), ffn2_w_down.astype(bf), row(ffn2_post)),
        mix_pre=row(mix_pre),
        w_lat=w_lat,
        w_conv=w_in[:, off_cb:off_g].astype(bf),
        w_g=w_in[:, off_g:].astype(bf),
        b_g=row(b_gate),
        q_norm=row(q_norm),
        kv_norm=row(kv_norm),
        w_qt=w_uq.T.astype(bf),
        w_uk=w_ukv3[:, :, :QK_NOPE].reshape(KV_LORA, N_HEADS * QK_NOPE).astype(bf),
        w_vt=w_ukv3[:, :, QK_NOPE:].reshape(KV_LORA, N_HEADS * V_HEAD).T.astype(bf),
        conv_w=conv_w,
        w_co=w_conv_out.astype(bf),
        w_o=w_o.astype(bf),
        mix_post=row(mix_post),
    )


def _layer(x, w, tables):
    batch, seq, d = x.shape
    x2 = x.reshape(batch * seq, d)
    h = _ffn(x2, *w["ffn1"])
    lat, cb, p = _in_proj(h, w["mix_pre"], w["w_lat"], w["w_conv"])
    qt, k, vt = _qkv(lat, batch, seq, w["q_norm"], w["kv_norm"], w["w_qt"], w["w_uk"],
                     w["w_vt"], *tables)
    a = _attention(qt, k, vt).reshape(batch * seq, d)
    h2 = _mix_out(h, a, cb, p, seq, w["mix_pre"], w["w_g"], w["b_g"], w["conv_w"],
                  w["w_co"], w["w_o"], w["mix_post"])
    y = _ffn(h2, *w["ffn2"])
    return y.reshape(batch, seq, d)


def kernel(x_prompt, x_sample, ffn1_pre, ffn1_w_gu, ffn1_w_down, ffn1_post, mix_pre, w_in, b_gate, q_norm, kv_norm, w_uq, w_ukv, conv_w, w_conv_out, w_o, mix_post, ffn2_pre, ffn2_w_gu, ffn2_w_down, ffn2_post):
    params = (ffn1_pre, ffn1_w_gu, ffn1_w_down, ffn1_post, mix_pre, w_in, b_gate, q_norm,
              kv_norm, w_uq, w_ukv, conv_w, w_conv_out, w_o, mix_post, ffn2_pre, ffn2_w_gu,
              ffn2_w_down, ffn2_post)
    depth = ffn1_pre.shape[0]
    tables = _rope_tables(max(x_prompt.shape[1], x_sample.shape[1]))
    y_prompt, y_sample = x_prompt, x_sample
    for l in range(depth):
        w = _prep_weights(*(t[l] for t in params))
        y_prompt = _layer(y_prompt, w, tables)
        y_sample = _layer(y_sample, w, tables)
    return (y_prompt, y_sample)
```

```python
import functools

import jax
import jax.numpy as jnp
from jax import lax
from jax.experimental import pallas as pl
from jax.experimental.pallas import tpu as pltpu

N_HEADS = 16
QK_NOPE = 128
QK_ROPE = 64
V_HEAD = 128
Q_LORA = 768
KV_LORA = 512
ROPE_THETA = 10000.0
D_CONV = 1024
EPS = 1e-6

QK_PAD = 256
LAT_PAD = 1408
KR_OFF = Q_LORA + KV_LORA
ROPE_HALF = QK_ROPE // 2
LANES = 128
V_EXT = V_HEAD + 16
NEG_BIG = -2.0 ** 100
LOG2E = 1.4426950408889634
MAX_EXCESS = 64.0

TM_FFN = 512
TF_FFN = 512
TM_IN = 512
TM_QKV = 256
TQ = 1024
TK = 512
ATTN_GROUP = 8
TM_MIX = 512
TC_MIX = 512
VMEM_LIMIT = 56 * 1024 * 1024

_NT = (((1,), (1,)), ((), ()))


def _rms(x, g):
    ms = jnp.mean(x * x, axis=-1, keepdims=True)
    return x * lax.rsqrt(ms + EPS) * g


def _sigmoid(x):
    return 1.0 / (1.0 + jnp.exp(-x))


def _dot(a, b):
    return jnp.dot(a, b, preferred_element_type=jnp.float32)


def _resident(shape):
    zeros = (0,) * len(shape)
    return pl.BlockSpec(shape, lambda *_: zeros, pipeline_mode=pl.Buffered(1))


def _ffn_kernel(x_ref, pre_ref, wg_ref, wu_ref, wd_ref, post_ref, o_ref, xn_ref, acc_ref):
    j = pl.program_id(1)
    last = pl.num_programs(1) - 1
    tm = x_ref.shape[0]
    halves = [pl.ds(r * (tm // 2), tm // 2) for r in range(2)]

    def chunk(xn):
        gate = _dot(xn, wg_ref[...])
        up = _dot(xn, wu_ref[...])
        act = (gate * _sigmoid(gate) * up).astype(jnp.bfloat16)
        return _dot(act, wd_ref[...])

    @pl.when(j == 0)
    def _():
        for rows in halves:
            xn = _rms(x_ref[rows, :], pre_ref[...]).astype(jnp.bfloat16)
            xn_ref[rows, :] = xn
            acc_ref[rows, :] = chunk(xn)

    @pl.when((j > 0) & (j < last))
    def _():
        acc_ref[...] += chunk(xn_ref[...])

    @pl.when(j == last)
    def _():
        for rows in halves:
            y = acc_ref[rows, :] + chunk(xn_ref[rows, :])
            o_ref[rows, :] = x_ref[rows, :] + 0.5 * _rms(y, post_ref[...])


def _ffn(x, pre, w_gu, w_down, post):
    n, d = x.shape
    d_ff = w_down.shape[0]
    tm, tf = TM_FFN, TF_FFN
    nj = d_ff // tf
    assert nj >= 2, nj
    return pl.pallas_call(
        _ffn_kernel,
        grid=(n // tm, nj),
        in_specs=[
            pl.BlockSpec((tm, d), lambda i, j: (i, 0)),
            pl.BlockSpec((1, d), lambda i, j: (0, 0)),
            pl.BlockSpec((d, tf), lambda i, j: (0, j)),
            pl.BlockSpec((d, tf), lambda i, j: (0, j + nj)),
            pl.BlockSpec((tf, d), lambda i, j: (j, 0)),
            pl.BlockSpec((1, d), lambda i, j: (0, 0)),
        ],
        out_specs=pl.BlockSpec((tm, d), lambda i, j: (i, 0)),
        out_shape=jax.ShapeDtypeStruct((n, d), jnp.float32),
        scratch_shapes=[pltpu.VMEM((tm, d), jnp.bfloat16), pltpu.VMEM((tm, d), jnp.float32)],
        compiler_params=pltpu.CompilerParams(
            dimension_semantics=("parallel", "arbitrary"), vmem_limit_bytes=VMEM_LIMIT),
        name="ffn",
    )(x, pre, w_gu, w_gu, w_down, post)


def _in_proj_kernel(h_ref, pre_ref, wlat_ref, wconv_ref, lat_ref, cb_ref, p_ref):
    u = _rms(h_ref[...], pre_ref[...]).astype(jnp.bfloat16)
    lat_ref[...] = _dot(u, wlat_ref[...])
    cb_ref[...] = _dot(u, wconv_ref[:, :D_CONV]).astype(jnp.bfloat16)
    cc = _dot(u, wconv_ref[:, D_CONV:2 * D_CONV])
    cx = _dot(u, wconv_ref[:, 2 * D_CONV:])
    p_ref[...] = (cc * cx).astype(jnp.bfloat16)


def _in_proj(h, pre, w_lat, w_conv):
    n, d = h.shape
    tm = TM_IN
    return pl.pallas_call(
        _in_proj_kernel,
        grid=(n // tm,),
        in_specs=[
            pl.BlockSpec((tm, d), lambda i: (i, 0)),
            _resident((1, d)),
            _resident(w_lat.shape),
            _resident(w_conv.shape),
        ],
        out_specs=[
            pl.BlockSpec((tm, LAT_PAD), lambda i: (i, 0)),
            pl.BlockSpec((tm, D_CONV), lambda i: (i, 0)),
            pl.BlockSpec((tm, D_CONV), lambda i: (i, 0)),
        ],
        out_shape=[
            jax.ShapeDtypeStruct((n, LAT_PAD), jnp.float32),
            jax.ShapeDtypeStruct((n, D_CONV), jnp.bfloat16),
            jax.ShapeDtypeStruct((n, D_CONV), jnp.bfloat16),
        ],
        compiler_params=pltpu.CompilerParams(
            dimension_semantics=("parallel",), vmem_limit_bytes=VMEM_LIMIT),
        name="in_proj",
    )(h, pre, w_lat, w_conv)


def _qkv_kernel(lat_ref, qn_ref, kvn_ref, wqt_ref, wuk_ref, wvt_ref, kc_ref, ks_ref,
                cost_ref, sint_ref, qt_ref, k_ref, vt_ref):
    tm = lat_ref.shape[0]
    qn = _rms(lat_ref[:, :Q_LORA], qn_ref[...]).astype(jnp.bfloat16)
    kvn = _rms(lat_ref[:, Q_LORA:KR_OFF], kvn_ref[...]).astype(jnp.bfloat16)

    kr = lat_ref[:, KR_OFF:]
    lane = lax.broadcasted_iota(jnp.int32, kr.shape, 1)
    swapped = jnp.where(lane < ROPE_HALF,
                        pltpu.roll(kr, LANES - ROPE_HALF, 1), pltpu.roll(kr, ROPE_HALF, 1))
    kro = (kr * kc_ref[...] + swapped * ks_ref[...]).astype(jnp.bfloat16)

    kn = _dot(kvn, wuk_ref[...]).astype(jnp.bfloat16)
    for h in range(N_HEADS):
        k_ref[0, h, :, :QK_NOPE] = kn[:, h * QK_NOPE:(h + 1) * QK_NOPE]
        k_ref[0, h, :, QK_NOPE:] = kro

    vt = lax.dot_general(wvt_ref[...], kvn, _NT, preferred_element_type=jnp.float32)
    vt_ref[0, 0, :, :V_HEAD, :] = vt.reshape(N_HEADS, V_HEAD, tm).astype(jnp.bfloat16)
    extra = lax.broadcasted_iota(jnp.int32, (N_HEADS, V_EXT - V_HEAD, tm), 1)
    vt_ref[0, 0, :, V_HEAD:, :] = jnp.where(extra == 0, 1.0, 0.0).astype(jnp.bfloat16)

    scale = (QK_NOPE + QK_ROPE) ** -0.5 * LOG2E
    qt =lax.dot_general(wqt_ref[...], qn, _NT, preferred_element_type=jnp.float32) * scale
    qt = qt.reshape(N_HEADS, QK_NOPE + QK_ROPE, tm)
    x1 = qt[:, QK_NOPE:QK_NOPE + ROPE_HALF, :]
    x2 = qt[:, QK_NOPE + ROPE_HALF:, :]
    cos = cost_ref[...][None]
    sin = sint_ref[...][None]
    qt_ref[0, :, :QK_NOPE, :] = qt[:, :QK_NOPE, :].astype(jnp.bfloat16)
    qt_ref[0, :, QK_NOPE:QK_NOPE + ROPE_HALF, :] = (x1 * cos - x2 * sin).astype(jnp.bfloat16)
    qt_ref[0, :, QK_NOPE + ROPE_HALF:QK_NOPE + QK_ROPE, :] = (x2 * cos + x1 * sin).astype(jnp.bfloat16)
    qt_ref[0, :, QK_NOPE + QK_ROPE:, :] = jnp.zeros(
        (N_HEADS, QK_PAD - QK_NOPE - QK_ROPE, tm), jnp.bfloat16)


def _qkv(lat, batch, seq, q_norm, kv_norm, w_qt, w_uk, w_vt, kcos, ksin, cos_t, sin_t):
    tm = TM_QKV
    ns = seq // tm
    per_chunk = TK // tm
    return pl.pallas_call(
        _qkv_kernel,
        grid=(batch, ns),
        in_specs=[
            pl.BlockSpec((tm, LAT_PAD), lambda b, s: (b * ns + s, 0)),
            _resident((1, Q_LORA)),
            _resident((1, KV_LORA)),
            _resident(w_qt.shape),
            _resident(w_uk.shape),
            _resident(w_vt.shape),
            pl.BlockSpec((tm, LANES), lambda b, s: (s, 0)),
            pl.BlockSpec((tm, LANES), lambda b, s: (s, 0)),
            pl.BlockSpec((ROPE_HALF, tm), lambda b, s: (0, s)),
            pl.BlockSpec((ROPE_HALF, tm), lambda b, s: (0, s)),
        ],
        out_specs=[
            pl.BlockSpec((1, N_HEADS, QK_PAD, tm), lambda b, s: (b, 0, 0, s)),
            pl.BlockSpec((1, N_HEADS, tm, QK_PAD), lambda b, s: (b, 0, s, 0)),
            pl.BlockSpec((1, 1, N_HEADS, V_EXT, tm),
                         lambda b, s: (b, s // per_chunk, 0, 0, s % per_chunk)),
        ],
        out_shape=[
            jax.ShapeDtypeStruct((batch, N_HEADS, QK_PAD, seq), jnp.bfloat16),
            jax.ShapeDtypeStruct((batch, N_HEADS, seq, QK_PAD), jnp.bfloat16),
            jax.ShapeDtypeStruct((batch, seq // TK, N_HEADS, V_EXT, TK), jnp.bfloat16),
        ],
        compiler_params=pltpu.CompilerParams(
            dimension_semantics=("parallel", "parallel"), vmem_limit_bytes=VMEM_LIMIT),
        name="qkv",
    )(lat, q_norm, kv_norm, w_qt, w_uk, w_vt, kcos, ksin, cos_t, sin_t)


def _attn_kernel(qt_ref, k_ref, vt_ref, o_ref, s0_ref, s1_ref, s2_ref, s3_ref, p0_ref, p1_ref,
                 acc_ref):
    nk = k_ref.shape[2]
    tq = qt_ref.shape[3]
    qt = qt_ref[0, 0]

    def scores(c, s_ref):
        s = _dot(k_ref[0, 0, c], qt).astype(jnp.bfloat16)
        s_ref[...] = s
        return jnp.max(s, axis=0, keepdims=True).astype(jnp.float32)

    def softmax(s_ref, p_ref, s_max, m_prev):
        m_new = jnp.maximum(m_prev, s_max)
        alpha = jnp.exp2(m_prev - m_new)
        p_ref[...] = jnp.exp2(s_ref[...].astype(jnp.float32) - m_new).astype(jnp.bfloat16)
        return m_new, alpha

    def apply(c, p_ref, alpha):
        acc_ref[...] = alpha * acc_ref[...] + _dot(vt_ref[0, c, 0], p_ref[...])

    s_refs = (s0_ref, s1_ref, s2_ref, s3_ref)
    p_refs = (p0_ref, p1_ref)
    ahead = 2
    group_len = ATTN_GROUP

    def group(i, carry, first, last):
        m, alpha, s_max, s_max_1 = carry
        for t in range(group_len):
            c = group_len * i + t
            s_max_2 = s_max_1
            if not (last and t + ahead >= group_len):
                s_max_2 = scores(c + ahead, s_refs[(t + ahead) % 4])
            if not (first and t == 0):
                apply(c - 1, p_refs[(t - 1) % 2], alpha)
            m, alpha = softmax(s_refs[t % 4], p_refs[t % 2], s_max, m)
            s_max, s_max_1 = s_max_1, s_max_2
        return m, alpha, s_max, s_max_1

    acc_ref[...] = jnp.zeros_like(acc_ref)
    s_max_0 = scores(0, s0_ref)
    s_max_1 = scores(1, s1_ref)
    carry = (jnp.full((1, tq), NEG_BIG, jnp.float32), jnp.ones((1, tq), jnp.float32),
             s_max_0, s_max_1)
    n_groups = nk // group_len
    if n_groups == 1:
        carry = group(0, carry, True, True)
    else:
        carry = group(0, carry, True, False)
        carry = lax.fori_loop(1, n_groups - 1, lambda i, c: group(i, c, False, False), carry)
        carry = group(n_groups - 1, carry, False, True)
    alpha = carry[1]
    apply(nk - 1, p_refs[(nk - 1) % 2], alpha)
    o_ref[0] = (acc_ref[:V_HEAD, :] / acc_ref[V_HEAD:V_HEAD + 1, :]).T.astype(jnp.bfloat16)


def _attn_fast_kernel(qt_ref, k_ref, vt_ref, o_ref, ex_ref, p0_ref, p1_ref, acc_ref):
    nk = k_ref.shape[2]
    qt = qt_ref[0, 0]
    p_refs = (p0_ref, p1_ref)

    s = _dot(k_ref[0, 0, 0], qt)
    ref = jnp.max(s, axis=0, keepdims=True)
    p0_ref[...] = jnp.exp2(s - ref).astype(jnp.bfloat16)
    acc_ref[...] = jnp.zeros_like(acc_ref)

    def apply(c, scale):
        acc_ref[...] = scale * acc_ref[...] + _dot(vt_ref[0, c, 0], p_refs[c % 2][...])

    def step(c, parity, carry):
        ref_prev, s_max_prev, scale_prev, excess = carry
        ref = jnp.maximum(ref_prev, s_max_prev)
        scale = jnp.exp2(ref_prev - ref)
        s = _dot(k_ref[0, 0, c], qt)
        p_refs[parity][...] = jnp.exp2(s - ref).astype(jnp.bfloat16)
        s_max = jnp.max(s, axis=0, keepdims=True)
        acc_ref[...] = scale_prev * acc_ref[...] + _dot(vt_ref[0, c - 1, 0],
                                                        p_refs[1 - parity][...])
        return ref, s_max, scale, jnp.maximum(excess, s_max - ref)

    def group(i, carry):
        for t in range(ATTN_GROUP):
            carry = step(1 + ATTN_GROUP * i + t, (1 + t) % 2, carry)
        return carry

    carry = (ref, ref, jnp.ones_like(ref), jnp.zeros_like(ref))
    n_groups = (nk - 1) // ATTN_GROUP
    carry = lax.fori_loop(0, n_groups, group, carry)
    for c in range(1 + ATTN_GROUP * n_groups, nk):
        carry = step(c, c % 2, carry)
    _, _, scale, excess = carry
    apply(nk - 1, scale)
    ex_ref[0, 0] = excess
    o_ref[0] = (acc_ref[:V_HEAD, :] / acc_ref[V_HEAD:V_HEAD + 1, :]).T.astype(jnp.bfloat16)


def _attention(qt, k, vt):
    batch, _, _, seq = qt.shape
    nk = seq // TK
    assert seq % TQ == 0 and nk % ATTN_GROUP == 0 and ATTN_GROUP % 4 == 0, (seq, TQ, TK)
    k = k.reshape(batch, N_HEADS, nk, TK, QK_PAD)
    grid = (batch, N_HEADS, seq // TQ)
    in_specs = [
        pl.BlockSpec((1, 1, QK_PAD, TQ), lambda b, h, q: (b, h, 0, q)),
        pl.BlockSpec((1, 1, nk, TK, QK_PAD), lambda b, h, q: (b, h, 0, 0, 0)),
        pl.BlockSpec((1, nk, 1, V_EXT, TK), lambda b, h, q: (b, 0, h, 0, 0)),
    ]
    out_spec = pl.BlockSpec((1, TQ, V_HEAD), lambda b, h, q: (b, q, h))
    out_shape = jax.ShapeDtypeStruct((batch, seq, N_HEADS * V_HEAD), jnp.bfloat16)
    params = pltpu.CompilerParams(
        dimension_semantics=("parallel", "parallel", "arbitrary"), vmem_limit_bytes=VMEM_LIMIT)

    def two_pass():
        return pl.pallas_call(
            _attn_kernel,
            grid=grid,
            in_specs=in_specs,
            out_specs=out_spec,
            out_shape=out_shape,
            scratch_shapes=[
                *[pltpu.VMEM((TK, TQ), jnp.bfloat16) for _ in range(6)],
                pltpu.VMEM((V_EXT, TQ), jnp.float32),
            ],
            compiler_params=params,
            name="attention_two_pass",
        )(qt, k, vt)

    fast, excess = pl.pallas_call(
        _attn_fast_kernel,
        grid=grid,
        in_specs=in_specs,
        out_specs=[out_spec, pl.BlockSpec((1, 1, 1, TQ), lambda b, h, q: (b, h, 0, q))],
        out_shape=[out_shape, jax.ShapeDtypeStruct((batch, N_HEADS, 1, seq), jnp.float32)],
        scratch_shapes=[
            pltpu.VMEM((TK, TQ), jnp.bfloat16), pltpu.VMEM((TK, TQ), jnp.bfloat16),
            pltpu.VMEM((V_EXT, TQ), jnp.float32),
        ],
        compiler_params=params,
        name="attention",
    )(qt, k, vt)
    return lax.cond(jnp.all(excess <= MAX_EXCESS), lambda: fast, two_pass)


def _mix_kernel(tiles_per_seq, h_ref, a_ref, cb_ref, p_ref, pprev_ref, pnext_ref, pre_ref,
                wga_ref, wgc_ref, bga_ref, bgc_ref, cw_ref, wco_ref, wo_ref, post_ref,
                o_ref, u_ref, cvin_ref, acc_ref):
    i = pl.program_id(0)
    j = pl.program_id(1)
    last = pl.num_programs(1) - 1
    tm = h_ref.shape[0]
    hm = tm // 2
    halves = [pl.ds(r * hm, hm) for r in range(2)]

    def prepare(r):
        rows = halves[r]
        u_ref[rows, :] = _rms(h_ref[rows, :], pre_ref[...]).astype(jnp.bfloat16)
        p = p_ref[rows, :].astype(jnp.float32)
        s_idx = i % tiles_per_seq
        if r == 0:
            prev_row = jnp.where(s_idx == 0, 0.0, pprev_ref[7:8, :].astype(jnp.float32))
            next_row = p_ref[hm:hm + 1, :].astype(jnp.float32)
        else:
            prev_row = p_ref[hm - 1:hm, :].astype(jnp.float32)
            next_row = jnp.where(s_idx == tiles_per_seq - 1, 0.0,
                                 pnext_ref[0:1, :].astype(jnp.float32))
        row = lax.broadcasted_iota(jnp.int32, p.shape, 0)
        p_before = jnp.where(row == 0, prev_row, pltpu.roll(p, 1, 0))
        p_after = jnp.where(row == hm - 1, next_row, pltpu.roll(p, hm - 1, 0))
        conv = p_before * cw_ref[0:1, :] + p * cw_ref[1:2, :] + p_after * cw_ref[2:3, :]
        cvin_ref[rows, :] = (cb_ref[rows, :].astype(jnp.float32) * conv).astype(jnp.bfloat16)

    def project(rows=slice(None)):
        u = u_ref[rows, :]
        g_a = _sigmoid(_dot(u, wga_ref[...]) + bga_ref[...])
        g_c = _sigmoid(_dot(u, wgc_ref[...]) + bgc_ref[...])
        cv = _dot(cvin_ref[rows, :], wco_ref[...])
        mixed = (g_a * a_ref[rows, :].astype(jnp.float32) + g_c * cv).astype(jnp.bfloat16)
        return _dot(mixed, wo_ref[...])

    @pl.when(j == 0)
    def _():
        for r, rows in enumerate(halves):
            prepare(r)
            acc_ref[rows, :] = project(rows)

    @pl.when((j > 0) & (j < last))
    def _():
        acc_ref[...] += project()

    @pl.when(j == last)
    def _():
        for rows in halves:
            y = acc_ref[rows, :] + project(rows)
            o_ref[rows, :] = h_ref[rows, :] + _rms(y, post_ref[...])


def _mix_out(h, a, cb, p, seq, pre, w_g, b_g, conv_w, w_co, w_o, post):
    n, d = h.shape
    tm, tc = TM_MIX, TC_MIX
    nj = d // tc
    assert nj >= 2, nj
    halo = 8
    per_tile = tm // halo
    last_halo = n // halo - 1
    return pl.pallas_call(
        functools.partial(_mix_kernel, seq // tm),
        grid=(n // tm, nj),
        in_specs=[
            pl.BlockSpec((tm, d), lambda i, j: (i, 0)),
            pl.BlockSpec((tm, tc), lambda i, j: (i, j)),
            pl.BlockSpec((tm, D_CONV), lambda i, j: (i, 0)),
            pl.BlockSpec((tm, D_CONV), lambda i, j: (i, 0)),
            pl.BlockSpec((halo, D_CONV), lambda i, j: (jnp.maximum(i * per_tile - 1, 0), 0)),
            pl.BlockSpec((halo, D_CONV), lambda i, j: (jnp.minimum((i + 1) * per_tile, last_halo), 0)),
            pl.BlockSpec((1, d), lambda i, j: (0, 0)),
            pl.BlockSpec((d, tc), lambda i, j: (0, j)),
            pl.BlockSpec((d, tc), lambda i, j: (0, j + nj)),
            pl.BlockSpec((1, tc), lambda i, j: (0, j)),
            pl.BlockSpec((1, tc), lambda i, j: (0, j + nj)),
            pl.BlockSpec((3, D_CONV), lambda i, j: (0, 0)),
            pl.BlockSpec((D_CONV, tc), lambda i, j: (0, j)),
            pl.BlockSpec((tc, d), lambda i, j: (j, 0)),
            pl.BlockSpec((1, d), lambda i, j: (0, 0)),
        ],
        out_specs=pl.BlockSpec((tm, d), lambda i, j: (i, 0)),
        out_shape=jax.ShapeDtypeStruct((n, d), jnp.float32),
        scratch_shapes=[
            pltpu.VMEM((tm, d), jnp.bfloat16),
            pltpu.VMEM((tm, D_CONV), jnp.bfloat16),
            pltpu.VMEM((tm, d), jnp.float32),
        ],
        compiler_params=pltpu.CompilerParams(
            dimension_semantics=("parallel", "arbitrary"), vmem_limit_bytes=VMEM_LIMIT),
        name="mix_out",
    )(h, a, cb, p, p, p, pre, w_g, w_g, b_g, b_g, conv_w, w_co, w_o, post)


def _rope_tables(seq):
    pos = jnp.arange(seq, dtype=jnp.float32)
    inv_freq = ROPE_THETA ** (-jnp.arange(0, QK_ROPE, 2, dtype=jnp.float32) / QK_ROPE)
    ang = pos[:, None] * inv_freq[None, :]
    cos, sin = jnp.cos(ang), jnp.sin(ang)
    zeros = jnp.zeros((seq, LANES - QK_ROPE), jnp.float32)
    kcos = jnp.concatenate([cos, cos, zeros], axis=1)
    ksin = jnp.concatenate([-sin, sin, zeros], axis=1)
    return kcos, ksin, cos.T, sin.T


def _prep_weights(ffn1_pre, ffn1_w_gu, ffn1_w_down, ffn1_post, mix_pre, w_in, b_gate, q_norm,
                  kv_norm, w_uq, w_ukv, conv_w, w_conv_out, w_o, mix_post, ffn2_pre, ffn2_w_gu,
                  ffn2_w_down, ffn2_post):
    bf = jnp.bfloat16
    d = w_in.shape[0]
    row = lambda v: v.reshape(1, -1)
    off_cb = KR_OFF + QK_ROPE
    off_g = off_cb + 3 * D_CONV
    w_lat = jnp.concatenate(
        [w_in[:, :off_cb], jnp.zeros((d, LAT_PAD - off_cb), w_in.dtype)], axis=1).astype(bf)
    w_ukv3 = w_ukv.reshape(KV_LORA, N_HEADS, QK_NOPE + V_HEAD)
    return dict(
        ffn1=(row(ffn1_pre), ffn1_w_gu.astype(bf), ffn1_w_down.astype(bf), row(ffn1_post)),
        ffn2=(row(ffn2_pre), ffn2_w_gu.astype(bf), ffn2_w_down.astype(bf), row(ffn2_post)),
        mix_pre=row(mix_pre),
        w_lat=w_lat,
        w_conv=w_in[:, off_cb:off_g].astype(bf),
        w_g=w_in[:, off_g:].astype(bf),
        b_g=row(b_gate),
        q_norm=row(q_norm),
        kv_norm=row(kv_norm),
        w_qt=w_uq.T.astype(bf),
        w_uk=w_ukv3[:, :, :QK_NOPE].reshape(KV_LORA, N_HEADS * QK_NOPE).astype(bf),
        w_vt=w_ukv3[:, :, QK_NOPE:].reshape(KV_LORA, N_HEADS * V_HEAD).T.astype(bf),
        conv_w=conv_w,
        w_co=w_conv_out.astype(bf),
        w_o=w_o.astype(bf),
        mix_post=row(mix_post),
    )


def _layer(x, w, tables):
    batch, seq, d = x.shape
    x2 = x.reshape(batch * seq, d)
    h = _ffn(x2, *w["ffn1"])
    lat, cb, p = _in_proj(h, w["mix_pre"], w["w_lat"], w["w_conv"])
    qt, k, vt = _qkv(lat, batch, seq, w["q_norm"], w["kv_norm"], w["w_qt"], w["w_uk"],
                     w["w_vt"], *tables)
    a = _attention(qt, k, vt).reshape(batch * seq, d)
    h2 = _mix_out(h, a, cb, p, seq, w["mix_pre"], w["w_g"], w["b_g"], w["conv_w"],
                  w["w_co"], w["w_o"], w["mix_post"])
    y = _ffn(h2, *w["ffn2"])
    return y.reshape(batch, seq, d)


def kernel(x_prompt, x_sample, ffn1_pre, ffn1_w_gu, ffn1_w_down, ffn1_post, mix_pre, w_in, b_gate, q_norm, kv_norm, w_uq, w_ukv, conv_w, w_conv_out, w_o, mix_post, ffn2_pre, ffn2_w_gu, ffn2_w_down, ffn2_post):
    params = (ffn1_pre, ffn1_w_gu, ffn1_w_down, ffn1_post, mix_pre, w_in, b_gate, q_norm,
              kv_norm, w_uq, w_ukv, conv_w, w_conv_out, w_o, mix_post, ffn2_pre, ffn2_w_gu,
              ffn2_w_down, ffn2_post)
    depth = ffn1_pre.shape[0]
    tables = _rope_tables(max(x_prompt.shape[1], x_sample.shape[1]))
    y_prompt, y_sample = x_prompt, x_sample
    for l in range(depth):
        w = _prep_weights(*(t[l] for t in params))
        y_prompt = _layer(y_prompt, w, tables)
        y_sample = _layer(y_sample, w, tables)
    return (y_prompt, y_sample)
```

```python
import functools

import jax
import jax.numpy as jnp
from jax import lax
from jax.experimental import pallas as pl
from jax.experimental.pallas import tpu as pltpu

N_HEADS = 16
QK_NOPE = 128
QK_ROPE = 64
V_HEAD = 128
Q_LORA = 768
KV_LORA = 512
ROPE_THETA = 10000.0
D_CONV = 1024
EPS = 1e-6

QK_PAD = 256
LAT_PAD = 1408
KR_OFF = Q_LORA + KV_LORA
ROPE_HALF = QK_ROPE // 2
LANES = 128
V_EXT = V_HEAD + 16
NEG_BIG = -2.0 ** 100
LOG2E = 1.4426950408889634
MAX_EXCESS = 64.0

TM_FFN = 512
TF_FFN = 512
TM_IN = 512
TM_QKV = 256
TQ = 1024
TK = 512
ATTN_GROUP = 8
TM_MIX = 512
TC_MIX = 512
VMEM_LIMIT = 56 * 1024 * 1024

_NT = (((1,), (1,)), ((), ()))


def _rms(x, g):
    ms = jnp.mean(x * x, axis=-1, keepdims=True)
    return x * lax.rsqrt(ms + EPS) * g


def _sigmoid(x):
    return 1.0 / (1.0 + jnp.exp(-x))


def _dot(a, b):
    return jnp.dot(a, b, preferred_element_type=jnp.float32)


def _resident(shape):
    zeros = (0,) * len(shape)
    return pl.BlockSpec(shape, lambda *_: zeros, pipeline_mode=pl.Buffered(1))


def _ffn_kernel(x_ref, pre_ref, wgu_ref, wd_ref, post_ref, o_ref, xn_ref, acc_ref):
    j = pl.program_id(1)
    last = pl.num_programs(1) - 1
    tm = x_ref.shape[0]
    tf = wd_ref.shape[0]
    halves = [pl.ds(r * (tm // 2), tm // 2) for r in range(2)]

    def chunk(xn):
        gate = _dot(xn, wgu_ref[0, :, :tf])
        up = _dot(xn, wgu_ref[0, :, tf:])
        act = (gate * _sigmoid(gate) * up).astype(jnp.bfloat16)
        return _dot(act, wd_ref[...])

    @pl.when(j == 0)
    def _():
        for rows in halves:
            xn = _rms(x_ref[rows, :], pre_ref[...]).astype(jnp.bfloat16)
            xn_ref[rows, :] = xn
            acc_ref[rows, :] = chunk(xn)

    @pl.when((j > 0) & (j < last))
    def _():
        acc_ref[...] += chunk(xn_ref[...])

    @pl.when(j == last)
    def _():
        for rows in halves:
            y = acc_ref[rows, :] + chunk(xn_ref[rows, :])
            o_ref[rows, :] = x_ref[rows, :] + 0.5 * _rms(y, post_ref[...])


def _ffn(x, pre, w_gu, w_down, post):
    n, d = x.shape
    nj, _, tf2 = w_gu.shape
    tm, tf = TM_FFN, tf2 // 2
    assert nj >= 2 and w_down.shape[0] == nj * tf, (w_gu.shape, w_down.shape)
    return pl.pallas_call(
        _ffn_kernel,
        grid=(n // tm, nj),
        in_specs=[
            pl.BlockSpec((tm, d), lambda i, j: (i, 0)),
            pl.BlockSpec((1, d), lambda i, j: (0, 0)),
            pl.BlockSpec((1, d, tf2), lambda i, j: (j, 0, 0)),
            pl.BlockSpec((tf, d), lambda i, j: (j, 0)),
            pl.BlockSpec((1, d), lambda i, j: (0, 0)),
        ],
        out_specs=pl.BlockSpec((tm, d), lambda i, j: (i, 0)),
        out_shape=jax.ShapeDtypeStruct((n, d), jnp.float32),
        scratch_shapes=[pltpu.VMEM((tm, d), jnp.bfloat16), pltpu.VMEM((tm, d), jnp.float32)],
        compiler_params=pltpu.CompilerParams(
            dimension_semantics=("parallel", "arbitrary"), vmem_limit_bytes=VMEM_LIMIT),
        name="ffn",
    )(x, pre, w_gu, w_down, post)


def _in_proj_kernel(h_ref, pre_ref, wlat_ref, wconv_ref, lat_ref, cb_ref, p_ref):
    u = _rms(h_ref[...], pre_ref[...]).astype(jnp.bfloat16)
    lat_ref[...] = _dot(u, wlat_ref[...])
    cb_ref[...] = _dot(u, wconv_ref[:, :D_CONV]).astype(jnp.bfloat16)
    cc = _dot(u, wconv_ref[:, D_CONV:2 * D_CONV])
    cx = _dot(u, wconv_ref[:, 2 * D_CONV:])
    p_ref[...] = (cc * cx).astype(jnp.bfloat16)


def _in_proj(h, pre, w_lat, w_conv):
    n, d = h.shape
    tm = TM_IN
    return pl.pallas_call(
        _in_proj_kernel,
        grid=(n // tm,),
        in_specs=[
            pl.BlockSpec((tm, d), lambda i: (i, 0)),
            _resident((1, d)),
            _resident(w_lat.shape),
            _resident(w_conv.shape),
        ],
        out_specs=[
            pl.BlockSpec((tm, LAT_PAD), lambda i: (i, 0)),
            pl.BlockSpec((tm, D_CONV), lambda i: (i, 0)),
            pl.BlockSpec((tm, D_CONV), lambda i: (i, 0)),
        ],
        out_shape=[
            jax.ShapeDtypeStruct((n, LAT_PAD), jnp.float32),
            jax.ShapeDtypeStruct((n, D_CONV), jnp.bfloat16),
            jax.ShapeDtypeStruct((n, D_CONV), jnp.bfloat16),
        ],
        compiler_params=pltpu.CompilerParams(
            dimension_semantics=("parallel",), vmem_limit_bytes=VMEM_LIMIT),
        name="in_proj",
    )(h, pre, w_lat, w_conv)


def _qkv_kernel(lat_ref, qn_ref, kvn_ref, wqt_ref, wuk_ref, wvt_ref, kc_ref, ks_ref,
                cost_ref, sint_ref, qt_ref, k_ref, vt_ref):
    tm = lat_ref.shape[0]
    qn = _rms(lat_ref[:, :Q_LORA], qn_ref[...]).astype(jnp.bfloat16)
    kvn = _rms(lat_ref[:, Q_LORA:KR_OFF], kvn_ref[...]).astype(jnp.bfloat16)

    kr = lat_ref[:, KR_OFF:]
    lane = lax.broadcasted_iota(jnp.int32, kr.shape, 1)
    swapped = jnp.where(lane < ROPE_HALF,
                        pltpu.roll(kr, LANES - ROPE_HALF, 1), pltpu.roll(kr, ROPE_HALF, 1))
    kro = (kr * kc_ref[...] + swapped * ks_ref[...]).astype(jnp.bfloat16)

    kn = _dot(kvn, wuk_ref[...]).astype(jnp.bfloat16)
    for h in range(N_HEADS):
        k_ref[0, h, :, :QK_NOPE] = kn[:, h * QK_NOPE:(h + 1) * QK_NOPE]
        k_ref[0, h, :, QK_NOPE:] = kro

    vt = lax.dot_general(wvt_ref[...], kvn, _NT, preferred_element_type=jnp.float32)
    vt_ref[0, 0, :, :V_HEAD, :] = vt.reshape(N_HEADS, V_HEAD, tm).astype(jnp.bfloat16)
    extra = lax.broadcasted_iota(jnp.int32, (N_HEADS, V_EXT - V_HEAD, tm), 1)
    vt_ref[0, 0, :, V_HEAD:, :] = jnp.where(extra == 0, 1.0, 0.0).astype(jnp.bfloat16)

    scale = (QK_NOPE + QK_ROPE) ** -0.5 * LOG2E
    qt =lax.dot_general(wqt_ref[...], qn, _NT, preferred_element_type=jnp.float32) * scale
    qt = qt.reshape(N_HEADS, QK_NOPE + QK_ROPE, tm)
    x1 = qt[:, QK_NOPE:QK_NOPE + ROPE_HALF, :]
    x2 = qt[:, QK_NOPE + ROPE_HALF:, :]
    cos = cost_ref[...][None]
    sin = sint_ref[...][None]
    qt_ref[0, :, :QK_NOPE, :] = qt[:, :QK_NOPE, :].astype(jnp.bfloat16)
    qt_ref[0, :, QK_NOPE:QK_NOPE + ROPE_HALF, :] = (x1 * cos - x2 * sin).astype(jnp.bfloat16)
    qt_ref[0, :, QK_NOPE + ROPE_HALF:QK_NOPE + QK_ROPE, :] = (x2 * cos + x1 * sin).astype(jnp.bfloat16)
    qt_ref[0, :, QK_NOPE + QK_ROPE:, :] = jnp.zeros(
        (N_HEADS, QK_PAD - QK_NOPE - QK_ROPE, tm), jnp.bfloat16)


def _qkv(lat, batch, seq, q_norm, kv_norm, w_qt, w_uk, w_vt, kcos, ksin, cos_t, sin_t):
    tm = TM_QKV
    ns = seq // tm
    per_chunk = TK // tm
    return pl.pallas_call(
        _qkv_kernel,
        grid=(batch, ns),
        in_specs=[
            pl.BlockSpec((tm, LAT_PAD), lambda b, s: (b * ns + s, 0)),
            _resident((1, Q_LORA)),
            _resident((1, KV_LORA)),
            _resident(w_qt.shape),
            _resident(w_uk.shape),
            _resident(w_vt.shape),
            pl.BlockSpec((tm, LANES), lambda b, s: (s, 0)),
            pl.BlockSpec((tm, LANES), lambda b, s: (s, 0)),
            pl.BlockSpec((ROPE_HALF, tm), lambda b, s: (0, s)),
            pl.BlockSpec((ROPE_HALF, tm), lambda b, s: (0, s)),
        ],
        out_specs=[
            pl.BlockSpec((1, N_HEADS, QK_PAD, tm), lambda b, s: (b, 0, 0, s)),
            pl.BlockSpec((1, N_HEADS, tm, QK_PAD), lambda b, s: (b, 0, s, 0)),
            pl.BlockSpec((1, 1, N_HEADS, V_EXT, tm),
                         lambda b, s: (b, s // per_chunk, 0, 0, s % per_chunk)),
        ],
        out_shape=[
            jax.ShapeDtypeStruct((batch, N_HEADS, QK_PAD, seq), jnp.bfloat16),
            jax.ShapeDtypeStruct((batch, N_HEADS, seq, QK_PAD), jnp.bfloat16),
            jax.ShapeDtypeStruct((batch, seq // TK, N_HEADS, V_EXT, TK), jnp.bfloat16),
        ],
        compiler_params=pltpu.CompilerParams(
            dimension_semantics=("parallel", "parallel"), vmem_limit_bytes=VMEM_LIMIT),
        name="qkv",
    )(lat, q_norm, kv_norm, w_qt, w_uk, w_vt, kcos, ksin, cos_t, sin_t)


def _attn_kernel(qt_ref, k_ref, vt_ref, o_ref, s0_ref, s1_ref, s2_ref, s3_ref, p0_ref, p1_ref,
                 acc_ref):
    nk = k_ref.shape[2]
    tq = qt_ref.shape[3]
    qt = qt_ref[0, 0]

    def scores(c, s_ref):
        s = _dot(k_ref[0, 0, c], qt).astype(jnp.bfloat16)
        s_ref[...] = s
        return jnp.max(s, axis=0, keepdims=True).astype(jnp.float32)

    def softmax(s_ref, p_ref, s_max, m_prev):
        m_new = jnp.maximum(m_prev, s_max)
        alpha = jnp.exp2(m_prev - m_new)
        p_ref[...] = jnp.exp2(s_ref[...].astype(jnp.float32) - m_new).astype(jnp.bfloat16)
        return m_new, alpha

    def apply(c, p_ref, alpha):
        acc_ref[...] = alpha * acc_ref[...] + _dot(vt_ref[0, c, 0], p_ref[...])

    s_refs = (s0_ref, s1_ref, s2_ref, s3_ref)
    p_refs = (p0_ref, p1_ref)
    ahead = 2
    group_len = ATTN_GROUP

    def group(i, carry, first, last):
        m, alpha, s_max, s_max_1 = carry
        for t in range(group_len):
            c = group_len * i + t
            s_max_2 = s_max_1
            if not (last and t + ahead >= group_len):
                s_max_2 = scores(c + ahead, s_refs[(t + ahead) % 4])
            if not (first and t == 0):
                apply(c - 1, p_refs[(t - 1) % 2], alpha)
            m, alpha = softmax(s_refs[t % 4], p_refs[t % 2], s_max, m)
            s_max, s_max_1 = s_max_1, s_max_2
        return m, alpha, s_max, s_max_1

    acc_ref[...] = jnp.zeros_like(acc_ref)
    s_max_0 = scores(0, s0_ref)
    s_max_1 = scores(1, s1_ref)
    carry = (jnp.full((1, tq), NEG_BIG, jnp.float32), jnp.ones((1, tq), jnp.float32),
             s_max_0, s_max_1)
    n_groups = nk // group_len
    if n_groups == 1:
        carry = group(0, carry, True, True)
    else:
        carry = group(0, carry, True, False)
        carry = lax.fori_loop(1, n_groups - 1, lambda i, c: group(i, c, False, False), carry)
        carry = group(n_groups - 1, carry, False, True)
    alpha = carry[1]
    apply(nk - 1, p_refs[(nk - 1) % 2], alpha)
    o_ref[0] = (acc_ref[:V_HEAD, :] / acc_ref[V_HEAD:V_HEAD + 1, :]).T.astype(jnp.bfloat16)


def _attn_fast_kernel(qt_ref, k_ref, vt_ref, o_ref, ex_ref, p0_ref, p1_ref, acc_ref):
    nk = k_ref.shape[2]
    qt = qt_ref[0, 0]
    p_refs = (p0_ref, p1_ref)

    s = _dot(k_ref[0, 0, 0], qt)
    ref = jnp.max(s, axis=0, keepdims=True)
    p0_ref[...] = jnp.exp2(s - ref).astype(jnp.bfloat16)
    acc_ref[...] = jnp.zeros_like(acc_ref)

    def apply(c, parity, scale):
        acc_ref[...] = scale * acc_ref[...] + _dot(vt_ref[0, c, 0], p_refs[parity][...])

    def step(c, parity, carry):
        ref_prev, s_max_prev, scale_prev, excess = carry
        ref = jnp.maximum(ref_prev, s_max_prev)
        scale = jnp.exp2(ref_prev - ref)
        s = _dot(k_ref[0, 0, c], qt)
        p_refs[parity][...] = jnp.exp2(s - ref).astype(jnp.bfloat16)
        s_max = jnp.max(s, axis=0, keepdims=True)
        apply(c - 1, 1 - parity, scale_prev)
        return ref, s_max, scale, jnp.maximum(excess, s_max - ref)

    group_len = ATTN_GROUP

    def group(i, carry):
        for t in range(group_len):
            carry = step(1 + group_len * i + t, (1 + t) % 2, carry)
        return carry

    carry = (ref, ref, jnp.ones_like(ref), jnp.zeros_like(ref))
    n_groups = (nk - 1) // group_len
    carry = lax.fori_loop(0, n_groups, group, carry)
    for c in range(1 + group_len * n_groups, nk):
        carry = step(c, c % 2, carry)
    _, _, scale, excess = carry
    apply(nk - 1, (nk - 1) % 2, scale)
    ex_ref[0, 0] = excess
    o_ref[0] = (acc_ref[:V_HEAD, :] / acc_ref[V_HEAD:V_HEAD + 1, :]).T.astype(jnp.bfloat16)


def _attention(qt, k, vt):
    batch, _, _, seq = qt.shape
    nk = seq // TK
    assert seq % TQ == 0 and nk % ATTN_GROUP == 0 and ATTN_GROUP % 4 == 0, (seq, TQ, TK)
    k = k.reshape(batch, N_HEADS, nk, TK, QK_PAD)
    grid = (batch, N_HEADS, seq // TQ)
    in_specs = [
        pl.BlockSpec((1, 1, QK_PAD, TQ), lambda b, h, q: (b, h, 0, q)),
        pl.BlockSpec((1, 1, nk, TK, QK_PAD), lambda b, h, q: (b, h, 0, 0, 0)),
        pl.BlockSpec((1, nk, 1, V_EXT, TK), lambda b, h, q: (b, 0, h, 0, 0)),
    ]
    out_spec = pl.BlockSpec((1, TQ, V_HEAD), lambda b, h, q: (b, q, h))
    out_shape = jax.ShapeDtypeStruct((batch, seq, N_HEADS * V_HEAD), jnp.bfloat16)
    params = pltpu.CompilerParams(
        dimension_semantics=("parallel", "parallel", "arbitrary"), vmem_limit_bytes=VMEM_LIMIT)

    def two_pass():
        return pl.pallas_call(
            _attn_kernel,
            grid=grid,
            in_specs=in_specs,
            out_specs=out_spec,
            out_shape=out_shape,
            scratch_shapes=[
                *[pltpu.VMEM((TK, TQ), jnp.bfloat16) for _ in range(6)],
                pltpu.VMEM((V_EXT, TQ), jnp.float32),
            ],
            compiler_params=params,
            name="attention_two_pass",
        )(qt, k, vt)

    fast, excess = pl.pallas_call(
        _attn_fast_kernel,
        grid=grid,
        in_specs=in_specs,
        out_specs=[out_spec, pl.BlockSpec((1, 1, 1, TQ), lambda b, h, q: (b, h, 0, q))],
        out_shape=[out_shape, jax.ShapeDtypeStruct((batch, N_HEADS, 1, seq), jnp.float32)],
        scratch_shapes=[
            pltpu.VMEM((TK, TQ), jnp.bfloat16), pltpu.VMEM((TK, TQ), jnp.bfloat16),
            pltpu.VMEM((V_EXT, TQ), jnp.float32),
        ],
        compiler_params=params,
        name="attention",
    )(qt, k, vt)
    return lax.cond(jnp.all(excess <= MAX_EXCESS), lambda: fast, two_pass)


def _mix_kernel(tiles_per_seq, h_ref, a_ref, cb_ref, p_ref, pprev_ref, pnext_ref, pre_ref,
                wg_ref, bg_ref, cw_ref, wco_ref, wo_ref, post_ref,
                o_ref, u_ref, cvin_ref, acc_ref):
    i = pl.program_id(0)
    j = pl.program_id(1)
    last = pl.num_programs(1) - 1
    tm = h_ref.shape[0]
    hm = tm // 2
    halves = [pl.ds(r * hm, hm) for r in range(2)]

    def prepare(r):
        rows = halves[r]
        u_ref[rows, :] = _rms(h_ref[rows, :], pre_ref[...]).astype(jnp.bfloat16)
        p = p_ref[rows, :].astype(jnp.float32)
        s_idx = i % tiles_per_seq
        if r == 0:
            prev_row = jnp.where(s_idx == 0, 0.0, pprev_ref[7:8, :].astype(jnp.float32))
            next_row = p_ref[hm:hm + 1, :].astype(jnp.float32)
        else:
            prev_row = p_ref[hm - 1:hm, :].astype(jnp.float32)
            next_row = jnp.where(s_idx == tiles_per_seq - 1, 0.0,
                                 pnext_ref[0:1, :].astype(jnp.float32))
        row = lax.broadcasted_iota(jnp.int32, p.shape, 0)
        p_before = jnp.where(row == 0, prev_row, pltpu.roll(p, 1, 0))
        p_after = jnp.where(row == hm - 1, next_row, pltpu.roll(p, hm - 1, 0))
        conv = p_before * cw_ref[0:1, :] + p * cw_ref[1:2, :] + p_after * cw_ref[2:3, :]
        cvin_ref[rows, :] = (cb_ref[rows, :].astype(jnp.float32) * conv).astype(jnp.bfloat16)

    def project(rows=slice(None)):
        u = u_ref[rows, :]
        tc = wo_ref.shape[0]
        g_a = _sigmoid(_dot(u, wg_ref[0, :, :tc]) + bg_ref[0, :, :tc])
        g_c = _sigmoid(_dot(u, wg_ref[0, :, tc:]) + bg_ref[0, :, tc:])
        cv = _dot(cvin_ref[rows, :], wco_ref[0])
        mixed = (g_a * a_ref[rows, :].astype(jnp.float32) + g_c * cv).astype(jnp.bfloat16)
        return _dot(mixed, wo_ref[...])

    @pl.when(j == 0)
    def _():
        for r, rows in enumerate(halves):
            prepare(r)
            acc_ref[rows, :] = project(rows)

    @pl.when((j > 0) & (j < last))
    def _():
        acc_ref[...] += project()

    @pl.when(j == last)
    def _():
        for rows in halves:
            y = acc_ref[rows, :] + project(rows)
            o_ref[rows, :] = h_ref[rows, :] + _rms(y, post_ref[...])


def _mix_out(h, a, cb, p, seq, pre, w_g, b_g, conv_w, w_co, w_o, post):
    n, d = h.shape
    tm, tc = TM_MIX, TC_MIX
    nj = d // tc
    assert nj >= 2 and w_g.shape == (nj, d, 2 * tc), (nj, w_g.shape)
    halo = 8
    per_tile = tm // halo
    last_halo = n // halo - 1
    return pl.pallas_call(
        functools.partial(_mix_kernel, seq // tm),
        grid=(n // tm, nj),
        in_specs=[
            pl.BlockSpec((tm, d), lambda i, j: (i, 0)),
            pl.BlockSpec((tm, tc), lambda i, j: (i, j)),
            pl.BlockSpec((tm, D_CONV), lambda i, j: (i, 0)),
            pl.BlockSpec((tm, D_CONV), lambda i, j: (i, 0)),
            pl.BlockSpec((halo, D_CONV), lambda i, j: (jnp.maximum(i * per_tile - 1, 0), 0)),
            pl.BlockSpec((halo, D_CONV), lambda i, j: (jnp.minimum((i + 1) * per_tile, last_halo), 0)),
            pl.BlockSpec((1, d), lambda i, j: (0, 0)),
            pl.BlockSpec((1, d, 2 * tc), lambda i, j: (j, 0, 0)),
            pl.BlockSpec((1, 1, 2 * tc), lambda i, j: (j, 0, 0)),
            pl.BlockSpec((3, D_CONV), lambda i, j: (0, 0)),
            pl.BlockSpec((1, D_CONV, tc), lambda i, j: (j, 0, 0)),
            pl.BlockSpec((tc, d), lambda i, j: (j, 0)),
            pl.BlockSpec((1, d), lambda i, j: (0, 0)),
        ],
        out_specs=pl.BlockSpec((tm, d), lambda i, j: (i, 0)),
        out_shape=jax.ShapeDtypeStruct((n, d), jnp.float32),
        scratch_shapes=[
            pltpu.VMEM((tm, d), jnp.bfloat16),
            pltpu.VMEM((tm, D_CONV), jnp.bfloat16),
            pltpu.VMEM((tm, d), jnp.float32),
        ],
        compiler_params=pltpu.CompilerParams(
            dimension_semantics=("parallel", "arbitrary"), vmem_limit_bytes=VMEM_LIMIT),
        name="mix_out",
    )(h, a, cb, p, p, p, pre, w_g, b_g, conv_w, w_co, w_o, post)


def _rope_tables(seq):
    pos = jnp.arange(seq, dtype=jnp.float32)
    inv_freq = ROPE_THETA ** (-jnp.arange(0, QK_ROPE, 2, dtype=jnp.float32) / QK_ROPE)
    ang = pos[:, None] * inv_freq[None, :]
    cos, sin = jnp.cos(ang), jnp.sin(ang)
    zeros = jnp.zeros((seq, LANES - QK_ROPE), jnp.float32)
    kcos = jnp.concatenate([cos, cos, zeros], axis=1)
    ksin = jnp.concatenate([-sin, sin, zeros], axis=1)
    return kcos, ksin, cos.T, sin.T


def _prep_weights(ffn1_pre, ffn1_w_gu, ffn1_w_down, ffn1_post, mix_pre, w_in, b_gate, q_norm,
                  kv_norm, w_uq, w_ukv, conv_w, w_conv_out, w_o, mix_post, ffn2_pre, ffn2_w_gu,
                  ffn2_w_down, ffn2_post):
    bf = jnp.bfloat16
    d = w_in.shape[0]
    row = lambda v: v.reshape(1, -1)
    off_cb = KR_OFF + QK_ROPE
    off_g = off_cb + 3 * D_CONV
    w_lat = jnp.concatenate(
        [w_in[:, :off_cb], jnp.zeros((d, LAT_PAD - off_cb), w_in.dtype)], axis=1).astype(bf)
    w_ukv3 = w_ukv.reshape(KV_LORA, N_HEADS, QK_NOPE + V_HEAD)

    def chunked_pair(w, t):
        r, half = w.shape[0], w.shape[1] // 2
        w = w.reshape(r, 2, half // t, t)
        return w.transpose(2, 0, 1, 3).reshape(half // t, r, 2 * t)

    def chunked_gu(w_gu):
        return chunked_pair(w_gu.astype(bf), TF_FFN)

    return dict(
        ffn1=(row(ffn1_pre), chunked_gu(ffn1_w_gu), ffn1_w_down.astype(bf), row(ffn1_post)),
        ffn2=(row(ffn2_pre), chunked_gu(ffn2_w_gu), ffn2_w_down.astype(bf), row(ffn2_post)),
        mix_pre=row(mix_pre),
        w_lat=w_lat,
        w_conv=w_in[:, off_cb:off_g].astype(bf),
        w_g=chunked_pair(w_in[:, off_g:].astype(bf), TC_MIX),
        b_g=chunked_pair(row(b_gate), TC_MIX),
        q_norm=row(q_norm),
        kv_norm=row(kv_norm),
        w_qt=w_uq.T.astype(bf),
        w_uk=w_ukv3[:, :, :QK_NOPE].reshape(KV_LORA, N_HEADS * QK_NOPE).astype(bf),
        w_vt=w_ukv3[:, :, QK_NOPE:].reshape(KV_LORA, N_HEADS * V_HEAD).T.astype(bf),
        conv_w=conv_w,
        w_co=w_conv_out.astype(bf).reshape(D_CONV, d // TC_MIX, TC_MIX).transpose(1, 0, 2),
        w_o=w_o.astype(bf),
        mix_post=row(mix_post),
    )


def _layer(x, w, tables):
    batch, seq, d = x.shape
    x2 = x.reshape(batch * seq, d)
    h = _ffn(x2, *w["ffn1"])
    lat, cb, p = _in_proj(h, w["mix_pre"], w["w_lat"], w["w_conv"])
    qt, k, vt = _qkv(lat, batch, seq, w["q_norm"], w["kv_norm"], w["w_qt"], w["w_uk"],
                     w["w_vt"], *tables)
    a = _attention(qt, k, vt).reshape(batch * seq, d)
    h2 = _mix_out(h, a, cb, p, seq, w["mix_pre"], w["w_g"], w["b_g"], w["conv_w"],
                  w["w_co"], w["w_o"], w["mix_post"])
    y = _ffn(h2, *w["ffn2"])
    return y.reshape(batch, seq, d)


def kernel(x_prompt, x_sample, ffn1_pre, ffn1_w_gu, ffn1_w_down, ffn1_post, mix_pre, w_in, b_gate, q_norm, kv_norm, w_uq, w_ukv, conv_w, w_conv_out, w_o, mix_post, ffn2_pre, ffn2_w_gu, ffn2_w_down, ffn2_post):
    params = (ffn1_pre, ffn1_w_gu, ffn1_w_down, ffn1_post, mix_pre, w_in, b_gate, q_norm,
              kv_norm, w_uq, w_ukv, conv_w, w_conv_out, w_o, mix_post, ffn2_pre, ffn2_w_gu,
              ffn2_w_down, ffn2_post)
    depth = ffn1_pre.shape[0]
    tables = _rope_tables(max(x_prompt.shape[1], x_sample.shape[1]))
    y_prompt, y_sample = x_prompt, x_sample
    for l in range(depth):
        w = _prep_weights(*(t[l] for t in params))
        y_prompt = _layer(y_prompt, w, tables)
        y_sample = _layer(y_sample, w, tables)
    return (y_prompt, y_sample)
```

```python
import functools

import jax
import jax.numpy as jnp
from jax import lax
from jax.experimental import pallas as pl
from jax.experimental.pallas import tpu as pltpu

N_HEADS = 16
QK_NOPE = 128
QK_ROPE = 64
V_HEAD = 128
Q_LORA = 768
KV_LORA = 512
ROPE_THETA = 10000.0
D_CONV = 1024
EPS = 1e-6

QK_PAD = 256
LAT_PAD = 1408
KR_OFF = Q_LORA + KV_LORA
ROPE_HALF = QK_ROPE // 2
LANES = 128
V_EXT = V_HEAD + 16
NEG_BIG = -2.0 ** 100
LOG2E = 1.4426950408889634
MAX_EXCESS = 64.0

TM_FFN = 512
TF_FFN = 512
TM_IN = 512
TM_QKV = 512
TQ = 1024
TK = 512
ATTN_GROUP = 8
FAST_GROUP = 8
FAST_CHUNKS_PER_STEP = 32
TM_MIX = 512
TC_MIX = 512
VMEM_LIMIT = 56 * 1024 * 1024

_NT = (((1,), (1,)), ((), ()))


def _rms(x, g):
    ms = jnp.mean(x * x, axis=-1, keepdims=True)
    return x * lax.rsqrt(ms + EPS) * g


def _sigmoid(x):
    return 1.0 / (1.0 + jnp.exp(-x))


def _dot(a, b):
    return jnp.dot(a, b, preferred_element_type=jnp.float32)


def _resident(shape):
    zeros = (0,) * len(shape)
    return pl.BlockSpec(shape, lambda *_: zeros, pipeline_mode=pl.Buffered(1))


def _ffn_kernel(x_ref, pre_ref, wg_ref, wu_ref, wd_ref, post_ref, o_ref, xn_ref, acc_ref):
    j = pl.program_id(1)
    last = pl.num_programs(1) - 1
    tm = x_ref.shape[0]
    halves = [pl.ds(r * (tm // 2), tm // 2) for r in range(2)]

    def chunk(xn):
        gate = _dot(xn, wg_ref[...])
        up = _dot(xn, wu_ref[...])
        act = (gate * _sigmoid(gate) * up).astype(jnp.bfloat16)
        return _dot(act, wd_ref[...])

    @pl.when(j == 0)
    def _():
        for rows in halves:
            xn = _rms(x_ref[rows, :], pre_ref[...]).astype(jnp.bfloat16)
            xn_ref[rows, :] = xn
            acc_ref[rows, :] = chunk(xn)

    @pl.when((j > 0) & (j < last))
    def _():
        acc_ref[...] += chunk(xn_ref[...])

    @pl.when(j == last)
    def _():
        for rows in halves:
            y = acc_ref[rows, :] + chunk(xn_ref[rows, :])
            o_ref[rows, :] = x_ref[rows, :] + 0.5 * _rms(y, post_ref[...])


def _ffn(x, pre, w_gu, w_down, post):
    n, d = x.shape
    d_ff = w_down.shape[0]
    tm, tf = TM_FFN, TF_FFN
    nj = d_ff // tf
    assert nj >= 2, nj
    return pl.pallas_call(
        _ffn_kernel,
        grid=(n // tm, nj),
        in_specs=[
            pl.BlockSpec((tm, d), lambda i, j: (i, 0)),
            pl.BlockSpec((1, d), lambda i, j: (0, 0)),
            pl.BlockSpec((d, tf), lambda i, j: (0, j)),
            pl.BlockSpec((d, tf), lambda i, j: (0, j + nj)),
            pl.BlockSpec((tf, d), lambda i, j: (j, 0)),
            pl.BlockSpec((1, d), lambda i, j: (0, 0)),
        ],
        out_specs=pl.BlockSpec((tm, d), lambda i, j: (i, 0)),
        out_shape=jax.ShapeDtypeStruct((n, d), jnp.float32),
        scratch_shapes=[pltpu.VMEM((tm, d), jnp.bfloat16), pltpu.VMEM((tm, d), jnp.float32)],
        compiler_params=pltpu.CompilerParams(
            dimension_semantics=("parallel", "arbitrary"), vmem_limit_bytes=VMEM_LIMIT),
        name="ffn",
    )(x, pre, w_gu, w_gu, w_down, post)


def _in_proj_kernel(h_ref, pre_ref, wlat_ref, wconv_ref, lat_ref, cb_ref, p_ref):
    u = _rms(h_ref[...], pre_ref[...]).astype(jnp.bfloat16)
    lat_ref[...] = _dot(u, wlat_ref[...])
    cb_ref[...] = _dot(u, wconv_ref[:, :D_CONV]).astype(jnp.bfloat16)
    cc = _dot(u, wconv_ref[:, D_CONV:2 * D_CONV])
    cx = _dot(u, wconv_ref[:, 2 * D_CONV:])
    p_ref[...] = (cc * cx).astype(jnp.bfloat16)


def _in_proj(h, pre, w_lat, w_conv):
    n, d = h.shape
    tm = TM_IN
    return pl.pallas_call(
        _in_proj_kernel,
        grid=(n // tm,),
        in_specs=[
            pl.BlockSpec((tm, d), lambda i: (i, 0)),
            _resident((1, d)),
            _resident(w_lat.shape),
            _resident(w_conv.shape),
        ],
        out_specs=[
            pl.BlockSpec((tm, LAT_PAD), lambda i: (i, 0)),
            pl.BlockSpec((tm, D_CONV), lambda i: (i, 0)),
            pl.BlockSpec((tm, D_CONV), lambda i: (i, 0)),
        ],
        out_shape=[
            jax.ShapeDtypeStruct((n, LAT_PAD), jnp.float32),
            jax.ShapeDtypeStruct((n, D_CONV), jnp.bfloat16),
            jax.ShapeDtypeStruct((n, D_CONV), jnp.bfloat16),
        ],
        compiler_params=pltpu.CompilerParams(
            dimension_semantics=("parallel",), vmem_limit_bytes=VMEM_LIMIT),
        name="in_proj",
    )(h, pre, w_lat, w_conv)


def _qkv_kernel(lat_ref, qn_ref, kvn_ref, wqt_ref, wuk_ref, wvt_ref, kc_ref, ks_ref,
                cost_ref, sint_ref, qt_ref, k_ref, vt_ref):
    tm = lat_ref.shape[0]
    qn = _rms(lat_ref[:, :Q_LORA], qn_ref[...]).astype(jnp.bfloat16)
    kvn = _rms(lat_ref[:, Q_LORA:KR_OFF], kvn_ref[...]).astype(jnp.bfloat16)

    kr = lat_ref[:, KR_OFF:]
    lane = lax.broadcasted_iota(jnp.int32, kr.shape, 1)
    swapped = jnp.where(lane < ROPE_HALF,
                        pltpu.roll(kr, LANES - ROPE_HALF, 1), pltpu.roll(kr, ROPE_HALF, 1))
    kro = (kr * kc_ref[...] + swapped * ks_ref[...]).astype(jnp.bfloat16)

    kn = _dot(kvn, wuk_ref[...]).astype(jnp.bfloat16)
    for h in range(N_HEADS):
        k_ref[0, h, :, :QK_NOPE] = kn[:, h * QK_NOPE:(h + 1) * QK_NOPE]
        k_ref[0, h, :, QK_NOPE:] = kro

    vt = lax.dot_general(wvt_ref[...], kvn, _NT, preferred_element_type=jnp.float32)
    vt_ref[0, 0, :, :V_HEAD, :] = vt.reshape(N_HEADS, V_HEAD, tm).astype(jnp.bfloat16)
    extra = lax.broadcasted_iota(jnp.int32, (N_HEADS, V_EXT - V_HEAD, tm), 1)
    vt_ref[0, 0, :, V_HEAD:, :] = jnp.where(extra == 0, 1.0, 0.0).astype(jnp.bfloat16)

    scale = (QK_NOPE + QK_ROPE) ** -0.5 * LOG2E
    qt =lax.dot_general(wqt_ref[...], qn, _NT, preferred_element_type=jnp.float32) * scale
    qt = qt.reshape(N_HEADS, QK_NOPE + QK_ROPE, tm)
    x1 = qt[:, QK_NOPE:QK_NOPE + ROPE_HALF, :]
    x2 = qt[:, QK_NOPE + ROPE_HALF:, :]
    cos = cost_ref[...][None]
    sin = sint_ref[...][None]
    qt_ref[0, :, :QK_NOPE, :] = qt[:, :QK_NOPE, :].astype(jnp.bfloat16)
    qt_ref[0, :, QK_NOPE:QK_NOPE + ROPE_HALF, :] = (x1 * cos - x2 * sin).astype(jnp.bfloat16)
    qt_ref[0, :, QK_NOPE + ROPE_HALF:QK_NOPE + QK_ROPE, :] = (x2 * cos + x1 * sin).astype(jnp.bfloat16)
    qt_ref[0, :, QK_NOPE + QK_ROPE:, :] = jnp.zeros(
        (N_HEADS, QK_PAD - QK_NOPE - QK_ROPE, tm), jnp.bfloat16)


def _qkv(lat, batch, seq, q_norm, kv_norm, w_qt, w_uk, w_vt, kcos, ksin, cos_t, sin_t):
    tm = TM_QKV
    ns = seq // tm
    per_chunk = TK // tm
    return pl.pallas_call(
        _qkv_kernel,
        grid=(batch, ns),
        in_specs=[
            pl.BlockSpec((tm, LAT_PAD), lambda b, s: (b * ns + s, 0)),
            _resident((1, Q_LORA)),
            _resident((1, KV_LORA)),
            _resident(w_qt.shape),
            _resident(w_uk.shape),
            _resident(w_vt.shape),
            pl.BlockSpec((tm, LANES), lambda b, s: (s, 0)),
            pl.BlockSpec((tm, LANES), lambda b, s: (s, 0)),
            pl.BlockSpec((ROPE_HALF, tm), lambda b, s: (0, s)),
            pl.BlockSpec((ROPE_HALF, tm), lambda b, s: (0, s)),
        ],
        out_specs=[
            pl.BlockSpec((1, N_HEADS, QK_PAD, tm), lambda b, s: (b, 0, 0, s)),
            pl.BlockSpec((1, N_HEADS, tm, QK_PAD), lambda b, s: (b, 0, s, 0)),
            pl.BlockSpec((1, 1, N_HEADS, V_EXT, tm),
                         lambda b, s: (b, s // per_chunk, 0, 0, s % per_chunk)),
        ],
        out_shape=[
            jax.ShapeDtypeStruct((batch, N_HEADS, QK_PAD, seq), jnp.bfloat16),
            jax.ShapeDtypeStruct((batch, N_HEADS, seq, QK_PAD), jnp.bfloat16),
            jax.ShapeDtypeStruct((batch, seq // TK, N_HEADS, V_EXT, TK), jnp.bfloat16),
        ],
        compiler_params=pltpu.CompilerParams(
            dimension_semantics=("parallel", "parallel"), vmem_limit_bytes=VMEM_LIMIT),
        name="qkv",
    )(lat, q_norm, kv_norm, w_qt, w_uk, w_vt, kcos, ksin, cos_t, sin_t)


def _attn_kernel(qt_ref, k_ref, vt_ref, o_ref, s0_ref, s1_ref, s2_ref, s3_ref, p0_ref, p1_ref,
                 acc_ref):
    nk = k_ref.shape[2]
    tq = qt_ref.shape[3]
    qt = qt_ref[0, 0]

    def scores(c, s_ref):
        s = _dot(k_ref[0, 0, c], qt).astype(jnp.bfloat16)
        s_ref[...] = s
        return jnp.max(s, axis=0, keepdims=True).astype(jnp.float32)

    def softmax(s_ref, p_ref, s_max, m_prev):
        m_new = jnp.maximum(m_prev, s_max)
        alpha = jnp.exp2(m_prev - m_new)
        p_ref[...] = jnp.exp2(s_ref[...].astype(jnp.float32) - m_new).astype(jnp.bfloat16)
        return m_new, alpha

    def apply(c, p_ref, alpha):
        acc_ref[...] = alpha * acc_ref[...] + _dot(vt_ref[0, c, 0], p_ref[...])

    s_refs = (s0_ref, s1_ref, s2_ref, s3_ref)
    p_refs = (p0_ref, p1_ref)
    ahead = 2
    group_len = ATTN_GROUP

    def group(i, carry, first, last):
        m, alpha, s_max, s_max_1 = carry
        for t in range(group_len):
            c = group_len * i + t
            s_max_2 = s_max_1
            if not (last and t + ahead >= group_len):
                s_max_2 = scores(c + ahead, s_refs[(t + ahead) % 4])
            if not (first and t == 0):
                apply(c - 1, p_refs[(t - 1) % 2], alpha)
            m, alpha = softmax(s_refs[t % 4], p_refs[t % 2], s_max, m)
            s_max, s_max_1 = s_max_1, s_max_2
        return m, alpha, s_max, s_max_1

    acc_ref[...] = jnp.zeros_like(acc_ref)
    s_max_0 = scores(0, s0_ref)
    s_max_1 = scores(1, s1_ref)
    carry = (jnp.full((1, tq), NEG_BIG, jnp.float32), jnp.ones((1, tq), jnp.float32),
             s_max_0, s_max_1)
    n_groups = nk // group_len
    if n_groups == 1:
        carry = group(0, carry, True, True)
    else:
        carry = group(0, carry, True, False)
        carry = lax.fori_loop(1, n_groups - 1, lambda i, c: group(i, c, False, False), carry)
        carry = group(n_groups - 1, carry, False, True)
    alpha = carry[1]
    apply(nk - 1, p_refs[(nk - 1) % 2], alpha)
    o_ref[0] = (acc_ref[:V_HEAD, :] / acc_ref[V_HEAD:V_HEAD + 1, :]).T.astype(jnp.bfloat16)


def _attn_fast_kernel(qt_ref, k_ref, vt_ref, o_ref, ex_ref, p0_ref, p1_ref, acc_ref):
    nk = k_ref.shape[2]
    tq = acc_ref.shape[2]
    p_refs = (p0_ref, p1_ref)
    group_len = FAST_GROUP

    def tile(sub):
        cols = pl.ds(sub * tq, tq)
        qt = qt_ref[0, 0, :, cols]
        acc = acc_ref.at[sub]

        s = _dot(k_ref[0, 0, 0], qt)
        ref = jnp.max(s, axis=0, keepdims=True)
        p0_ref[...] = jnp.exp2(s - ref).astype(jnp.bfloat16)
        acc[...] = jnp.zeros_like(acc)

        def apply(c, parity, scale):
            acc[...] = scale * acc[...] + _dot(vt_ref[0, c, 0], p_refs[parity][...])

        def step(c, parity, carry):
            ref_prev, s_max_prev, scale_prev, excess = carry
            ref = jnp.maximum(ref_prev, s_max_prev)
            scale = jnp.exp2(ref_prev - ref)
            s = _dot(k_ref[0, 0, c], qt)
            p_refs[parity][...] = jnp.exp2(s - ref).astype(jnp.bfloat16)
            s_max = jnp.max(s, axis=0, keepdims=True)
            apply(c - 1, 1 - parity, scale_prev)
            return ref, s_max, scale, jnp.maximum(excess, s_max - ref)

        def group(i, carry):
            for t in range(group_len):
                carry = step(1 + group_len * i + t, (1 + t) % 2, carry)
            return carry

        carry = (ref, ref, jnp.ones_like(ref), jnp.zeros_like(ref))
        n_groups = (nk - 1) // group_len
        carry = lax.fori_loop(0, n_groups, group, carry)
        for c in range(1 + group_len * n_groups, nk):
            carry = step(c, c % 2, carry)
        _, _, scale, excess = carry
        apply(nk - 1, (nk - 1) % 2, scale)
        ex_ref[0, 0, :, cols] = excess
        o_ref[0, cols, :] = (acc[:V_HEAD, :] / acc[V_HEAD:V_HEAD + 1, :]).T.astype(jnp.bfloat16)

    for sub in range(acc_ref.shape[0]):
        tile(sub)


def _attention(qt, k, vt):
    batch, _, _, seq = qt.shape
    nk = seq // TK
    assert seq % TQ == 0 and nk % ATTN_GROUP == 0 and ATTN_GROUP % 4 == 0, (seq, TQ, TK)
    k = k.reshape(batch, N_HEADS, nk, TK, QK_PAD)
    grid = (batch, N_HEADS, seq // TQ)
    in_specs = [
        pl.BlockSpec((1, 1, QK_PAD, TQ), lambda b, h, q: (b, h, 0, q)),
        pl.BlockSpec((1, 1, nk, TK, QK_PAD), lambda b, h, q: (b, h, 0, 0, 0)),
        pl.BlockSpec((1, nk, 1, V_EXT, TK), lambda b, h, q: (b, 0, h, 0, 0)),
    ]
    out_spec = pl.BlockSpec((1, TQ, V_HEAD), lambda b, h, q: (b, q, h))
    out_shape = jax.ShapeDtypeStruct((batch, seq, N_HEADS * V_HEAD), jnp.bfloat16)
    params = pltpu.CompilerParams(
        dimension_semantics=("parallel", "parallel", "arbitrary"), vmem_limit_bytes=VMEM_LIMIT)

    def two_pass():
        return pl.pallas_call(
            _attn_kernel,
            grid=grid,
            in_specs=in_specs,
            out_specs=out_spec,
            out_shape=out_shape,
            scratch_shapes=[
                *[pltpu.VMEM((TK, TQ), jnp.bfloat16) for _ in range(6)],
                pltpu.VMEM((V_EXT, TQ), jnp.float32),
            ],
            compiler_params=params,
            name="attention_two_pass",
        )(qt, k, vt)

    n_sub = max(1, min(seq // TQ, FAST_CHUNKS_PER_STEP // nk))
    assert (seq // TQ) % n_sub == 0, (seq, TQ, n_sub)
    tqs = n_sub * TQ
    fast, excess = pl.pallas_call(
        _attn_fast_kernel,
        grid=(batch, N_HEADS, seq // tqs),
        in_specs=[pl.BlockSpec((1, 1, QK_PAD, tqs), lambda b, h, q: (b, h, 0, q))] + in_specs[1:],
        out_specs=[pl.BlockSpec((1, tqs, V_HEAD), lambda b, h, q: (b, q, h)),
                   pl.BlockSpec((1, 1, 1, tqs), lambda b, h, q: (b, h, 0, q))],
        out_shape=[out_shape, jax.ShapeDtypeStruct((batch, N_HEADS, 1, seq), jnp.float32)],
        scratch_shapes=[
            pltpu.VMEM((TK, TQ), jnp.bfloat16), pltpu.VMEM((TK, TQ), jnp.bfloat16),
            pltpu.VMEM((n_sub, V_EXT, TQ), jnp.float32),
        ],
        compiler_params=params,
        name="attention",
    )(qt, k, vt)
    return lax.cond(jnp.all(excess <= MAX_EXCESS), lambda: fast, two_pass)


def _mix_kernel(tiles_per_seq, h_ref, a_ref, cb_ref, p_ref, pprev_ref, pnext_ref, pre_ref,
                wga_ref, wgc_ref, bga_ref, bgc_ref, cw_ref, wco_ref, wo_ref, post_ref,
                o_ref, u_ref, cvin_ref, acc_ref):
    i = pl.program_id(0)
    j = pl.program_id(1)
    last = pl.num_programs(1) - 1
    tm = h_ref.shape[0]
    hm = tm // 2
    halves = [pl.ds(r * hm, hm) for r in range(2)]

    def prepare(r):
        rows = halves[r]
        u_ref[rows, :] = _rms(h_ref[rows, :], pre_ref[...]).astype(jnp.bfloat16)
        p = p_ref[rows, :].astype(jnp.float32)
        s_idx = i % tiles_per_seq
        if r == 0:
            prev_row = jnp.where(s_idx == 0, 0.0, pprev_ref[7:8, :].astype(jnp.float32))
            next_row = p_ref[hm:hm + 1, :].astype(jnp.float32)
        else:
            prev_row = p_ref[hm - 1:hm, :].astype(jnp.float32)
            next_row = jnp.where(s_idx == tiles_per_seq - 1, 0.0,
                                 pnext_ref[0:1, :].astype(jnp.float32))
        row = lax.broadcasted_iota(jnp.int32, p.shape, 0)
        p_before = jnp.where(row == 0, prev_row, pltpu.roll(p, 1, 0))
        p_after = jnp.where(row == hm - 1, next_row, pltpu.roll(p, hm - 1, 0))
        conv = p_before * cw_ref[0:1, :] + p * cw_ref[1:2, :] + p_after * cw_ref[2:3, :]
        cvin_ref[rows, :] = (cb_ref[rows, :].astype(jnp.float32) * conv).astype(jnp.bfloat16)

    def project(rows=slice(None)):
        u = u_ref[rows, :]
        g_a = _sigmoid(_dot(u, wga_ref[...]) + bga_ref[...])
        g_c = _sigmoid(_dot(u, wgc_ref[...]) + bgc_ref[...])
        cv = _dot(cvin_ref[rows, :], wco_ref[...])
        mixed = (g_a * a_ref[rows, :].astype(jnp.float32) + g_c * cv).astype(jnp.bfloat16)
        return _dot(mixed, wo_ref[...])

    @pl.when(j == 0)
    def _():
        for r, rows in enumerate(halves):
            prepare(r)
            acc_ref[rows, :] = project(rows)

    @pl.when((j > 0) & (j < last))
    def _():
        acc_ref[...] += project()

    @pl.when(j == last)
    def _():
        for rows in halves:
            y = acc_ref[rows, :] + project(rows)
            o_ref[rows, :] = h_ref[rows, :] + _rms(y, post_ref[...])


def _mix_out(h, a, cb, p, seq, pre, w_g, b_g, conv_w, w_co, w_o, post):
    n, d = h.shape
    tm, tc = TM_MIX, TC_MIX
    nj = d // tc
    assert nj >= 2, nj
    halo = 8
    per_tile = tm // halo
    last_halo = n // halo - 1
    return pl.pallas_call(
        functools.partial(_mix_kernel, seq // tm),
        grid=(n // tm, nj),
        in_specs=[
            pl.BlockSpec((tm, d), lambda i, j: (i, 0)),
            pl.BlockSpec((tm, tc), lambda i, j: (i, j)),
            pl.BlockSpec((tm, D_CONV), lambda i, j: (i, 0)),
            pl.BlockSpec((tm, D_CONV), lambda i, j: (i, 0)),
            pl.BlockSpec((halo, D_CONV), lambda i, j: (jnp.maximum(i * per_tile - 1, 0), 0)),
            pl.BlockSpec((halo, D_CONV), lambda i, j: (jnp.minimum((i + 1) * per_tile, last_halo), 0)),
            pl.BlockSpec((1, d), lambda i, j: (0, 0)),
            pl.BlockSpec((d, tc), lambda i, j: (0, j)),
            pl.BlockSpec((d, tc), lambda i, j: (0, j + nj)),
            pl.BlockSpec((1, tc), lambda i, j: (0, j)),
            pl.BlockSpec((1, tc), lambda i, j: (0, j + nj)),
            pl.BlockSpec((3, D_CONV), lambda i, j: (0, 0)),
            pl.BlockSpec((D_CONV, tc), lambda i, j: (0, j)),
            pl.BlockSpec((tc, d), lambda i, j: (j, 0)),
            pl.BlockSpec((1, d), lambda i, j: (0, 0)),
        ],
        out_specs=pl.BlockSpec((tm, d), lambda i, j: (i, 0)),
        out_shape=jax.ShapeDtypeStruct((n, d), jnp.float32),
        scratch_shapes=[
            pltpu.VMEM((tm, d), jnp.bfloat16),
            pltpu.VMEM((tm, D_CONV), jnp.bfloat16),
            pltpu.VMEM((tm, d), jnp.float32),
        ],
        compiler_params=pltpu.CompilerParams(
            dimension_semantics=("parallel", "arbitrary"), vmem_limit_bytes=VMEM_LIMIT),
        name="mix_out",
    )(h, a, cb, p, p, p, pre, w_g, w_g, b_g, b_g, conv_w, w_co, w_o, post)


def _rope_tables(seq):
    pos = jnp.arange(seq, dtype=jnp.float32)
    inv_freq = ROPE_THETA ** (-jnp.arange(0, QK_ROPE, 2, dtype=jnp.float32) / QK_ROPE)
    ang = pos[:, None] * inv_freq[None, :]
    cos, sin = jnp.cos(ang), jnp.sin(ang)
    zeros = jnp.zeros((seq, LANES - QK_ROPE), jnp.float32)
    kcos = jnp.concatenate([cos, cos, zeros], axis=1)
    ksin = jnp.concatenate([-sin, sin, zeros], axis=1)
    return kcos, ksin, cos.T, sin.T


def _prep_weights(ffn1_pre, ffn1_w_gu, ffn1_w_down, ffn1_post, mix_pre, w_in, b_gate, q_norm,
                  kv_norm, w_uq, w_ukv, conv_w, w_conv_out, w_o, mix_post, ffn2_pre, ffn2_w_gu,
                  ffn2_w_down, ffn2_post):
    bf = jnp.bfloat16
    d = w_in.shape[0]
    row = lambda v: v.reshape(1, -1)
    off_cb = KR_OFF + QK_ROPE
    off_g = off_cb + 3 * D_CONV
    w_lat = jnp.concatenate(
        [w_in[:, :off_cb], jnp.zeros((d, LAT_PAD - off_cb), w_in.dtype)], axis=1).astype(bf)
    w_ukv3 = w_ukv.reshape(KV_LORA, N_HEADS, QK_NOPE + V_HEAD)
    return dict(
        ffn1=(row(ffn1_pre), ffn1_w_gu.astype(bf), ffn1_w_down.astype(bf), row(ffn1_post)),
        ffn2=(row(ffn2_pre), ffn2_w_gu.astype(bf), ffn2_w_down.astype(bf), row(ffn2_post)),
        mix_pre=row(mix_pre),
        w_lat=w_lat,
        w_conv=w_in[:, off_cb:off_g].astype(bf),
        w_g=w_in[:, off_g:].astype(bf),
        b_g=row(b_gate),
        q_norm=row(q_norm),
        kv_norm=row(kv_norm),
        w_qt=w_uq.T.astype(bf),
        w_uk=w_ukv3[:, :, :QK_NOPE].reshape(KV_LORA, N_HEADS * QK_NOPE).astype(bf),
        w_vt=w_ukv3[:, :, QK_NOPE:].reshape(KV_LORA, N_HEADS * V_HEAD).T.astype(bf),
        conv_w=conv_w,
        w_co=w_conv_out.astype(bf),
        w_o=w_o.astype(bf),
        mix_post=row(mix_post),
    )


def _layer(x, w, tables):
    batch, seq, d = x.shape
    x2 = x.reshape(batch * seq, d)
    h = _ffn(x2, *w["ffn1"])
    lat, cb, p = _in_proj(h, w["mix_pre"], w["w_lat"], w["w_conv"])
    qt, k, vt = _qkv(lat, batch, seq, w["q_norm"], w["kv_norm"], w["w_qt"], w["w_uk"],
                     w["w_vt"], *tables)
    a = _attention(qt, k, vt).reshape(batch * seq, d)
    h2 = _mix_out(h, a, cb, p, seq, w["mix_pre"], w["w_g"], w["b_g"], w["conv_w"],
                  w["w_co"], w["w_o"], w["mix_post"])
    y = _ffn(h2, *w["ffn2"])
    return y.reshape(batch, seq, d)


def kernel(x_prompt, x_sample, ffn1_pre, ffn1_w_gu, ffn1_w_down, ffn1_post, mix_pre, w_in, b_gate, q_norm, kv_norm, w_uq, w_ukv, conv_w, w_conv_out, w_o, mix_post, ffn2_pre, ffn2_w_gu, ffn2_w_down, ffn2_post):
    params = (ffn1_pre, ffn1_w_gu, ffn1_w_down, ffn1_post, mix_pre, w_in, b_gate, q_norm,
              kv_norm, w_uq, w_ukv, conv_w, w_conv_out, w_o, mix_post, ffn2_pre, ffn2_w_gu,
              ffn2_w_down, ffn2_post)
    depth = ffn1_pre.shape[0]
    tables = _rope_tables(max(x_prompt.shape[1], x_sample.shape[1]))
    y_prompt, y_sample = x_prompt, x_sample
    for l in range(depth):
        w = _prep_weights(*(t[l] for t in params))
        y_prompt = _layer(y_prompt, w, tables)
        y_sample = _layer(y_sample, w, tables)
    return (y_prompt, y_sample)
```

```python
import functools

import jax
import jax.numpy as jnp
from jax import lax
from jax.experimental import pallas as pl
from jax.experimental.pallas import tpu as pltpu

N_HEADS = 16
QK_NOPE = 128
QK_ROPE = 64
V_HEAD = 128
Q_LORA = 768
KV_LORA = 512
ROPE_THETA = 10000.0
D_CONV = 1024
EPS = 1e-6

QK_PAD = 192
LAT_PAD = 1408
KR_OFF = Q_LORA + KV_LORA
ROPE_HALF = QK_ROPE // 2
LANES = 128
V_EXT = V_HEAD + 16
NEG_BIG = -2.0 ** 100
LOG2E = 1.4426950408889634
MAX_EXCESS = 64.0

TM_FFN = 512
TF_FFN = 512
TM_IN = 512
TM_QKV = 512
TQ = 1024
TK = 512
ATTN_GROUP = 8
FAST_GROUP = 8
FAST_CHUNKS_PER_STEP = 64
TM_MIX = 512
TC_MIX = 512
VMEM_LIMIT = 56 * 1024 * 1024

_NT = (((1,), (1,)), ((), ()))


def _rms(x, g):
    ms = jnp.mean(x * x, axis=-1, keepdims=True)
    return x * lax.rsqrt(ms + EPS) * g


def _sigmoid(x):
    return 1.0 / (1.0 + jnp.exp(-x))


def _dot(a, b):
    return jnp.dot(a, b, preferred_element_type=jnp.float32)


def _resident(shape):
    zeros = (0,) * len(shape)
    return pl.BlockSpec(shape, lambda *_: zeros, pipeline_mode=pl.Buffered(1))


def _ffn_kernel(x_ref, pre_ref, wg_ref, wu_ref, wd_ref, post_ref, o_ref, xn_ref, acc_ref):
    j = pl.program_id(1)
    last = pl.num_programs(1) - 1
    tm = x_ref.shape[0]
    halves = [pl.ds(r * (tm // 2), tm // 2) for r in range(2)]

    def chunk(xn):
        gate = _dot(xn, wg_ref[...])
        up = _dot(xn, wu_ref[...])
        act = (gate * _sigmoid(gate) * up).astype(jnp.bfloat16)
        return _dot(act, wd_ref[...])

    @pl.when(j == 0)
    def _():
        for rows in halves:
            xn = _rms(x_ref[rows, :], pre_ref[...]).astype(jnp.bfloat16)
            xn_ref[rows, :] = xn
            acc_ref[rows, :] = chunk(xn)

    @pl.when((j > 0) & (j < last))
    def _():
        acc_ref[...] += chunk(xn_ref[...])

    @pl.when(j == last)
    def _():
        for rows in halves:
            y = acc_ref[rows, :] + chunk(xn_ref[rows, :])
            o_ref[rows, :] = x_ref[rows, :] + 0.5 * _rms(y, post_ref[...])


def _ffn(x, pre, w_gu, w_down, post):
    n, d = x.shape
    d_ff = w_down.shape[0]
    tm, tf = TM_FFN, TF_FFN
    nj = d_ff // tf
    assert nj >= 2, nj
    def stream_kernel(x_ref, pre_ref, wgu_hbm, wd_hbm, post_ref, o_ref, xn_ref, acc_ref):
        xn_ref[...] = _rms(x_ref[...], pre_ref[...]).astype(jnp.bfloat16)
        acc_ref[...] = jnp.zeros_like(acc_ref)

        def chunk(wg_ref, wu_ref, wd_ref):
            xn = xn_ref[...]
            gate = _dot(xn, wg_ref[...])
            up = _dot(xn, wu_ref[...])
            act = (gate * _sigmoid(gate) * up).astype(jnp.bfloat16)
            acc_ref[...] += _dot(act, wd_ref[...])

        deep = pl.Buffered(3)
        pltpu.emit_pipeline(
            chunk,
            grid=(nj,),
            in_specs=[
                pl.BlockSpec((d, tf), lambda j: (0, j), pipeline_mode=deep),
                pl.BlockSpec((d, tf), lambda j: (0, j + nj), pipeline_mode=deep),
                pl.BlockSpec((tf, d), lambda j: (j, 0), pipeline_mode=deep),
            ],
        )(wgu_hbm, wgu_hbm, wd_hbm)
        o_ref[...] = x_ref[...] + 0.5 * _rms(acc_ref[...], post_ref[...])

    return pl.pallas_call(
        stream_kernel,
        grid=(n // tm,),
        in_specs=[
            pl.BlockSpec((tm, d), lambda i: (i, 0)),
            pl.BlockSpec((1, d), lambda i: (0, 0)),
            pl.BlockSpec(memory_space=pl.ANY),
            pl.BlockSpec(memory_space=pl.ANY),
            pl.BlockSpec((1, d), lambda i: (0, 0)),
        ],
        out_specs=pl.BlockSpec((tm, d), lambda i: (i, 0)),
        out_shape=jax.ShapeDtypeStruct((n, d), jnp.float32),
        scratch_shapes=[pltpu.VMEM((tm, d), jnp.bfloat16), pltpu.VMEM((tm, d), jnp.float32)],
        compiler_params=pltpu.CompilerParams(
            dimension_semantics=("arbitrary",), vmem_limit_bytes=VMEM_LIMIT),
        name="ffn",
    )(x, pre, w_gu, w_down, post)


def _in_proj_kernel(h_ref, pre_ref, wlat_ref, wconv_ref, lat_ref, cb_ref, p_ref):
    u = _rms(h_ref[...], pre_ref[...]).astype(jnp.bfloat16)
    lat_ref[...] = _dot(u, wlat_ref[...])
    cb_ref[...] = _dot(u, wconv_ref[:, :D_CONV]).astype(jnp.bfloat16)
    cc = _dot(u, wconv_ref[:, D_CONV:2 * D_CONV])
    cx = _dot(u, wconv_ref[:, 2 * D_CONV:])
    p_ref[...] = (cc * cx).astype(jnp.bfloat16)


def _in_proj(h, pre, w_lat, w_conv):
    n, d = h.shape
    tm = TM_IN
    return pl.pallas_call(
        _in_proj_kernel,
        grid=(n // tm,),
        in_specs=[
            pl.BlockSpec((tm, d), lambda i: (i, 0)),
            _resident((1, d)),
            _resident(w_lat.shape),
            _resident(w_conv.shape),
        ],
        out_specs=[
            pl.BlockSpec((tm, LAT_PAD), lambda i: (i, 0)),
            pl.BlockSpec((tm, D_CONV), lambda i: (i, 0)),
            pl.BlockSpec((tm, D_CONV), lambda i: (i, 0)),
        ],
        out_shape=[
            jax.ShapeDtypeStruct((n, LAT_PAD), jnp.float32),
            jax.ShapeDtypeStruct((n, D_CONV), jnp.bfloat16),
            jax.ShapeDtypeStruct((n, D_CONV), jnp.bfloat16),
        ],
        compiler_params=pltpu.CompilerParams(
            dimension_semantics=("parallel",), vmem_limit_bytes=VMEM_LIMIT),
        name="in_proj",
    )(h, pre, w_lat, w_conv)


def _qkv_kernel(lat_ref, qn_ref, kvn_ref, wqt_ref, wuk_ref, wvt_ref, kc_ref, ks_ref,
                cost_ref, sint_ref, qt_ref, k_ref, vt_ref):
    tm = lat_ref.shape[0]
    qn = _rms(lat_ref[:, :Q_LORA], qn_ref[...]).astype(jnp.bfloat16)
    kvn = _rms(lat_ref[:, Q_LORA:KR_OFF], kvn_ref[...]).astype(jnp.bfloat16)

    kr = lat_ref[:, KR_OFF:]
    lane = lax.broadcasted_iota(jnp.int32, kr.shape, 1)
    swapped = jnp.where(lane < ROPE_HALF,
                        pltpu.roll(kr, LANES - ROPE_HALF, 1), pltpu.roll(kr, ROPE_HALF, 1))
    kro = (kr * kc_ref[...] + swapped * ks_ref[...]).astype(jnp.bfloat16)

    kn = _dot(kvn, wuk_ref[...]).astype(jnp.bfloat16)
    for h in range(N_HEADS):
        k_ref[0, h, :, :QK_NOPE] = kn[:, h * QK_NOPE:(h + 1) * QK_NOPE]
        k_ref[0, h, :, QK_NOPE:] = kro[:, :QK_PAD - QK_NOPE]

    vt = lax.dot_general(wvt_ref[...], kvn, _NT, preferred_element_type=jnp.float32)
    vt_ref[0, 0, :, :V_HEAD, :] = vt.reshape(N_HEADS, V_HEAD, tm).astype(jnp.bfloat16)
    extra = lax.broadcasted_iota(jnp.int32, (N_HEADS, V_EXT - V_HEAD, tm), 1)
    vt_ref[0, 0, :, V_HEAD:, :] = jnp.where(extra == 0, 1.0, 0.0).astype(jnp.bfloat16)

    scale = (QK_NOPE + QK_ROPE) ** -0.5 * LOG2E
    qt = lax.dot_general(wqt_ref[...], qn, _NT, preferred_element_type=jnp.float32) * scale
    qt = qt.reshape(N_HEADS, QK_NOPE + QK_ROPE, tm)
    x1 = qt[:, QK_NOPE:QK_NOPE + ROPE_HALF, :]
    x2 = qt[:, QK_NOPE + ROPE_HALF:, :]
    cos = cost_ref[...][None]
    sin = sint_ref[...][None]
    qt_ref[0, :, :QK_NOPE, :] = qt[:, :QK_NOPE, :].astype(jnp.bfloat16)
    qt_ref[0, :, QK_NOPE:QK_NOPE + ROPE_HALF, :] = (x1 * cos - x2 * sin).astype(jnp.bfloat16)
    qt_ref[0, :, QK_NOPE + ROPE_HALF:QK_NOPE + QK_ROPE, :] = (x2 * cos + x1 * sin).astype(jnp.bfloat16)
    if QK_PAD > QK_NOPE + QK_ROPE:
        qt_ref[0, :, QK_NOPE + QK_ROPE:, :] = jnp.zeros(
            (N_HEADS, QK_PAD - QK_NOPE - QK_ROPE, tm), jnp.bfloat16)


def _qkv(lat, batch, seq, q_norm, kv_norm, w_qt, w_uk, w_vt, kcos, ksin, cos_t, sin_t):
    tm = TM_QKV
    ns = seq // tm
    per_chunk = TK // tm
    return pl.pallas_call(
        _qkv_kernel,
        grid=(batch, ns),
        in_specs=[
            pl.BlockSpec((tm, LAT_PAD), lambda b, s: (b * ns + s, 0)),
            _resident((1, Q_LORA)),
            _resident((1, KV_LORA)),
            _resident(w_qt.shape),
            _resident(w_uk.shape),
            _resident(w_vt.shape),
            pl.BlockSpec((tm, LANES), lambda b, s: (s, 0)),
            pl.BlockSpec((tm, LANES), lambda b, s: (s, 0)),
            pl.BlockSpec((ROPE_HALF, tm), lambda b, s: (0, s)),
            pl.BlockSpec((ROPE_HALF, tm), lambda b, s: (0, s)),
        ],
        out_specs=[
            pl.BlockSpec((1, N_HEADS, QK_PAD, tm), lambda b, s: (b, 0, 0, s)),
            pl.BlockSpec((1, N_HEADS, tm, QK_PAD), lambda b, s: (b, 0, s, 0)),
            pl.BlockSpec((1, 1, N_HEADS, V_EXT, tm),
                         lambda b, s: (b, s // per_chunk, 0, 0, s % per_chunk)),
        ],
        out_shape=[
            jax.ShapeDtypeStruct((batch, N_HEADS, QK_PAD, seq), jnp.bfloat16),
            jax.ShapeDtypeStruct((batch, N_HEADS, seq, QK_PAD), jnp.bfloat16),
            jax.ShapeDtypeStruct((batch, seq // TK, N_HEADS, V_EXT, TK), jnp.bfloat16),
        ],
        compiler_params=pltpu.CompilerParams(
            dimension_semantics=("parallel", "parallel"), vmem_limit_bytes=VMEM_LIMIT),
        name="qkv",
    )(lat, q_norm, kv_norm, w_qt, w_uk, w_vt, kcos, ksin, cos_t, sin_t)


def _attn_kernel(qt_ref, k_ref, vt_ref, o_ref, s0_ref, s1_ref, s2_ref, s3_ref, p0_ref, p1_ref,
                 acc_ref):
    nk = k_ref.shape[2]
    tq = qt_ref.shape[3]
    qt = qt_ref[0, 0]

    def scores(c, s_ref):
        s = _dot(k_ref[0, 0, c], qt).astype(jnp.bfloat16)
        s_ref[...] = s
        return jnp.max(s, axis=0, keepdims=True).astype(jnp.float32)

    def softmax(s_ref, p_ref, s_max, m_prev):
        m_new = jnp.maximum(m_prev, s_max)
        alpha = jnp.exp2(m_prev - m_new)
        p_ref[...] = jnp.exp2(s_ref[...].astype(jnp.float32) - m_new).astype(jnp.bfloat16)
        return m_new, alpha

    def apply(c, p_ref, alpha):
        acc_ref[...] = alpha * acc_ref[...] + _dot(vt_ref[0, c, 0], p_ref[...])

    s_refs = (s0_ref, s1_ref, s2_ref, s3_ref)
    p_refs = (p0_ref, p1_ref)
    ahead = 2
    group_len = ATTN_GROUP

    def group(i, carry, first, last):
        m, alpha, s_max, s_max_1 = carry
        for t in range(group_len):
            c = group_len * i + t
            s_max_2 = s_max_1
            if not (last and t + ahead >= group_len):
                s_max_2 = scores(c + ahead, s_refs[(t + ahead) % 4])
            if not (first and t == 0):
                apply(c - 1, p_refs[(t - 1) % 2], alpha)
            m, alpha = softmax(s_refs[t % 4], p_refs[t % 2], s_max, m)
            s_max, s_max_1 = s_max_1, s_max_2
        return m, alpha, s_max, s_max_1

    acc_ref[...] = jnp.zeros_like(acc_ref)
    s_max_0 = scores(0, s0_ref)
    s_max_1 = scores(1, s1_ref)
    carry = (jnp.full((1, tq), NEG_BIG, jnp.float32), jnp.ones((1, tq), jnp.float32),
             s_max_0, s_max_1)
    n_groups = nk // group_len
    if n_groups == 1:
        carry = group(0, carry, True, True)
    else:
        carry = group(0, carry, True, False)
        carry = lax.fori_loop(1, n_groups - 1, lambda i, c: group(i, c, False, False), carry)
        carry = group(n_groups - 1, carry, False, True)
    alpha = carry[1]
    apply(nk - 1, p_refs[(nk - 1) % 2], alpha)
    o_ref[0] = (acc_ref[:V_HEAD, :] / acc_ref[V_HEAD:V_HEAD + 1, :]).T.astype(jnp.bfloat16)


def _attn_fast_kernel(qt_ref, k_ref, vt_ref, o_ref, ex_ref, p0_ref, p1_ref, acc_ref):
    nk = k_ref.shape[2]
    tq = acc_ref.shape[2]
    p_refs = (p0_ref, p1_ref)
    group_len = FAST_GROUP

    def tile(sub):
        cols = pl.ds(sub * tq, tq)
        qt = qt_ref[0, 0, :, cols]
        acc = acc_ref.at[sub]

        s = _dot(k_ref[0, 0, 0], qt)
        ref = jnp.max(s, axis=0, keepdims=True)
        p0_ref[...] = jnp.exp2(s - ref).astype(jnp.bfloat16)
        acc[...] = jnp.zeros_like(acc)

        def apply(c, parity, scale):
            acc[...] = scale * acc[...] + _dot(vt_ref[0, c, 0], p_refs[parity][...])

        def step(c, parity, carry):
            ref_prev, s_max_prev, scale_prev, excess = carry
            ref = jnp.maximum(ref_prev, s_max_prev)
            scale = jnp.exp2(ref_prev - ref)
            s = _dot(k_ref[0, 0, c], qt)
            p_refs[parity][...] = jnp.exp2(s - ref).astype(jnp.bfloat16)
            s_max = jnp.max(s, axis=0, keepdims=True)
            apply(c - 1, 1 - parity, scale_prev)
            return ref, s_max, scale, jnp.maximum(excess, s_max - ref)

        def group(i, carry):
            for t in range(group_len):
                carry = step(1 + group_len * i + t, (1 + t) % 2, carry)
            return carry

        carry = (ref, ref, jnp.ones_like(ref), jnp.zeros_like(ref))
        n_groups = (nk - 1) // group_len
        carry = lax.fori_loop(0, n_groups, group, carry)
        for c in range(1 + group_len * n_groups, nk):
            carry = step(c, c % 2, carry)
        _, _, scale, excess = carry
        apply(nk - 1, (nk - 1) % 2, scale)
        ex_ref[0, 0, :, cols] = excess
        o_ref[0, cols, :] = (acc[:V_HEAD, :] / acc[V_HEAD:V_HEAD + 1, :]).T.astype(jnp.bfloat16)

    for sub in range(acc_ref.shape[0]):
        tile(sub)


def _attention(qt, k, vt):
    batch, _, _, seq = qt.shape
    nk = seq // TK
    assert seq % TQ == 0 and nk % ATTN_GROUP == 0 and ATTN_GROUP % 4 == 0, (seq, TQ, TK)
    k = k.reshape(batch, N_HEADS, nk, TK, QK_PAD)
    grid = (batch, N_HEADS, seq // TQ)
    in_specs = [
        pl.BlockSpec((1, 1, QK_PAD, TQ), lambda b, h, q: (b, h, 0, q)),
        pl.BlockSpec((1, 1, nk, TK, QK_PAD), lambda b, h, q: (b, h, 0, 0, 0)),
        pl.BlockSpec((1, nk, 1, V_EXT, TK), lambda b, h, q: (b, 0, h, 0, 0)),
    ]
    out_spec = pl.BlockSpec((1, TQ, V_HEAD), lambda b, h, q: (b, q, h))
    out_shape = jax.ShapeDtypeStruct((batch, seq, N_HEADS * V_HEAD), jnp.bfloat16)
    params = pltpu.CompilerParams(
        dimension_semantics=("parallel", "parallel", "arbitrary"), vmem_limit_bytes=VMEM_LIMIT)

    def two_pass():
        return pl.pallas_call(
            _attn_kernel,
            grid=grid,
            in_specs=in_specs,
            out_specs=out_spec,
            out_shape=out_shape,
            scratch_shapes=[
                *[pltpu.VMEM((TK, TQ), jnp.bfloat16) for _ in range(6)],
                pltpu.VMEM((V_EXT, TQ), jnp.float32),
            ],
            compiler_params=params,
            name="attention_two_pass",
        )(qt, k, vt)

    n_sub = max(1, min(seq // TQ, FAST_CHUNKS_PER_STEP // nk))
    assert (seq // TQ) % n_sub == 0, (seq, TQ, n_sub)
    tqs = n_sub * TQ
    fast, excess = pl.pallas_call(
        _attn_fast_kernel,
        grid=(batch, N_HEADS, seq // tqs),
        in_specs=[pl.BlockSpec((1, 1, QK_PAD, tqs), lambda b, h, q: (b, h, 0, q))] + in_specs[1:],
        out_specs=[pl.BlockSpec((1, tqs, V_HEAD), lambda b, h, q: (b, q, h)),
                   pl.BlockSpec((1, 1, 1, tqs), lambda b, h, q: (b, h, 0, q))],
        out_shape=[out_shape, jax.ShapeDtypeStruct((batch, N_HEADS, 1, seq), jnp.float32)],
        scratch_shapes=[
            pltpu.VMEM((TK, TQ), jnp.bfloat16), pltpu.VMEM((TK, TQ), jnp.bfloat16),
            pltpu.VMEM((n_sub, V_EXT, TQ), jnp.float32),
        ],
        compiler_params=params,
        name="attention",
    )(qt, k, vt)
    return lax.cond(jnp.all(excess <= MAX_EXCESS), lambda: fast, two_pass)


def _mix_kernel(tiles_per_seq, h_ref, a_ref, cb_ref, p_ref, pprev_ref, pnext_ref, pre_ref,
                wga_ref, wgc_ref, bga_ref, bgc_ref, cw_ref, wco_ref, wo_ref, post_ref,
                o_ref, u_ref, cvin_ref, acc_ref):
    i = pl.program_id(0)
    j = pl.program_id(1)
    last = pl.num_programs(1) - 1
    tm = h_ref.shape[0]
    hm = tm // 2
    halves = [pl.ds(r * hm, hm) for r in range(2)]

    def prepare(r):
        rows = halves[r]
        u_ref[rows, :] = _rms(h_ref[rows, :], pre_ref[...]).astype(jnp.bfloat16)
        p = p_ref[rows, :].astype(jnp.float32)
        s_idx = i % tiles_per_seq
        if r == 0:
            prev_row = jnp.where(s_idx == 0, 0.0, pprev_ref[7:8, :].astype(jnp.float32))
            next_row = p_ref[hm:hm + 1, :].astype(jnp.float32)
        else:
            prev_row = p_ref[hm - 1:hm, :].astype(jnp.float32)
            next_row = jnp.where(s_idx == tiles_per_seq - 1, 0.0,
                                 pnext_ref[0:1, :].astype(jnp.float32))
        row = lax.broadcasted_iota(jnp.int32, p.shape, 0)
        p_before = jnp.where(row == 0, prev_row, pltpu.roll(p, 1, 0))
        p_after = jnp.where(row == hm - 1, next_row, pltpu.roll(p, hm - 1, 0))
        conv = p_before * cw_ref[0:1, :] + p * cw_ref[1:2, :] + p_after * cw_ref[2:3, :]
        cvin_ref[rows, :] = (cb_ref[rows, :].astype(jnp.float32) * conv).astype(jnp.bfloat16)

    def project(rows=slice(None)):
        u = u_ref[rows, :]
        g_a = _sigmoid(_dot(u, wga_ref[...]) + bga_ref[...])
        g_c = _sigmoid(_dot(u, wgc_ref[...]) + bgc_ref[...])
        cv = _dot(cvin_ref[rows, :], wco_ref[...])
        mixed = (g_a * a_ref[rows, :].astype(jnp.float32) + g_c * cv).astype(jnp.bfloat16)
        return _dot(mixed, wo_ref[...])

    @pl.when(j == 0)
    def _():
        for r, rows in enumerate(halves):
            prepare(r)
            acc_ref[rows, :] = project(rows)

    @pl.when((j > 0) & (j < last))
    def _():
        acc_ref[...] += project()

    @pl.when(j == last)
    def _():
        for rows in halves:
            y = acc_ref[rows, :] + project(rows)
            o_ref[rows, :] = h_ref[rows, :] + _rms(y, post_ref[...])


def _mix_out(h, a, cb, p, seq, pre, w_g, b_g, conv_w, w_co, w_o, post):
    n, d = h.shape
    tm, tc = TM_MIX, TC_MIX
    nj = d // tc
    assert nj >= 2, nj
    halo = 8
    per_tile = tm // halo
    last_halo = n // halo - 1
    return pl.pallas_call(
        functools.partial(_mix_kernel, seq // tm),
        grid=(n // tm, nj),
        in_specs=[
            pl.BlockSpec((tm, d), lambda i, j: (i, 0)),
            pl.BlockSpec((tm, tc), lambda i, j: (i, j)),
            pl.BlockSpec((tm, D_CONV), lambda i, j: (i, 0)),
            pl.BlockSpec((tm, D_CONV), lambda i, j: (i, 0)),
            pl.BlockSpec((halo, D_CONV), lambda i, j: (jnp.maximum(i * per_tile - 1, 0), 0)),
            pl.BlockSpec((halo, D_CONV), lambda i, j: (jnp.minimum((i + 1) * per_tile, last_halo), 0)),
            pl.BlockSpec((1, d), lambda i, j: (0, 0)),
            pl.BlockSpec((d, tc), lambda i, j: (0, j)),
            pl.BlockSpec((d, tc), lambda i, j: (0, j + nj)),
            pl.BlockSpec((1, tc), lambda i, j: (0, j)),
            pl.BlockSpec((1, tc), lambda i, j: (0, j + nj)),
            pl.BlockSpec((3, D_CONV), lambda i, j: (0, 0)),
            pl.BlockSpec((D_CONV, tc), lambda i, j: (0, j)),
            pl.BlockSpec((tc, d), lambda i, j: (j, 0)),
            pl.BlockSpec((1, d), lambda i, j: (0, 0)),
        ],
        out_specs=pl.BlockSpec((tm, d), lambda i, j: (i, 0)),
        out_shape=jax.ShapeDtypeStruct((n, d), jnp.float32),
        scratch_shapes=[
            pltpu.VMEM((tm, d), jnp.bfloat16),
            pltpu.VMEM((tm, D_CONV), jnp.bfloat16),
            pltpu.VMEM((tm, d), jnp.float32),
        ],
        compiler_params=pltpu.CompilerParams(
            dimension_semantics=("parallel", "arbitrary"), vmem_limit_bytes=VMEM_LIMIT),
        name="mix_out",
    )(h, a, cb, p, p, p, pre, w_g, w_g, b_g, b_g, conv_w, w_co, w_o, post)


def _rope_tables(seq):
    pos = jnp.arange(seq, dtype=jnp.float32)
    inv_freq = ROPE_THETA ** (-jnp.arange(0, QK_ROPE, 2, dtype=jnp.float32) / QK_ROPE)
    ang = pos[:, None] * inv_freq[None, :]
    cos, sin = jnp.cos(ang), jnp.sin(ang)
    zeros = jnp.zeros((seq, LANES - QK_ROPE), jnp.float32)
    kcos = jnp.concatenate([cos, cos, zeros], axis=1)
    ksin = jnp.concatenate([-sin, sin, zeros], axis=1)
    return kcos, ksin, cos.T, sin.T


def _prep_weights(ffn1_pre, ffn1_w_gu, ffn1_w_down, ffn1_post, mix_pre, w_in, b_gate, q_norm,
                  kv_norm, w_uq, w_ukv, conv_w, w_conv_out, w_o, mix_post, ffn2_pre, ffn2_w_gu,
                  ffn2_w_down, ffn2_post):
    bf = jnp.bfloat16
    d = w_in.shape[0]
    row = lambda v: v.reshape(1, -1)
    off_cb = KR_OFF + QK_ROPE
    off_g = off_cb + 3 * D_CONV
    w_lat = jnp.concatenate(
        [w_in[:, :off_cb], jnp.zeros((d, LAT_PAD - off_cb), w_in.dtype)], axis=1).astype(bf)
    w_ukv3 = w_ukv.reshape(KV_LORA, N_HEADS, QK_NOPE + V_HEAD)
    return dict(
        ffn1=(row(ffn1_pre), ffn1_w_gu.astype(bf), ffn1_w_down.astype(bf), row(ffn1_post)),
        ffn2=(row(ffn2_pre), ffn2_w_gu.astype(bf), ffn2_w_down.astype(bf), row(ffn2_post)),
        mix_pre=row(mix_pre),
        w_lat=w_lat,
        w_conv=w_in[:, off_cb:off_g].astype(bf),
        w_g=w_in[:, off_g:].astype(bf),
        b_g=row(b_gate),
        q_norm=row(q_norm),
        kv_norm=row(kv_norm),
        w_qt=w_uq.T.astype(bf),
        w_uk=w_ukv3[:, :, :QK_NOPE].reshape(KV_LORA, N_HEADS * QK_NOPE).astype(bf),
        w_vt=w_ukv3[:, :, QK_NOPE:].reshape(KV_LORA, N_HEADS * V_HEAD).T.astype(bf),
        conv_w=conv_w,
        w_co=w_conv_out.astype(bf),
        w_o=w_o.astype(bf),
        mix_post=row(mix_post),
    )


def _layer(x, w, tables):
    batch, seq, d = x.shape
    x2 = x.reshape(batch * seq, d)
    h = _ffn(x2, *w["ffn1"])
    lat, cb, p = _in_proj(h, w["mix_pre"], w["w_lat"], w["w_conv"])
    qt, k, vt = _qkv(lat, batch, seq, w["q_norm"], w["kv_norm"], w["w_qt"], w["w_uk"],
                     w["w_vt"], *tables)
    a = _attention(qt, k, vt).reshape(batch * seq, d)
    h2 = _mix_out(h, a, cb, p, seq, w["mix_pre"], w["w_g"], w["b_g"], w["conv_w"],
                  w["w_co"], w["w_o"], w["mix_post"])
    y = _ffn(h2, *w["ffn2"])
    return y.reshape(batch, seq, d)


def kernel(x_prompt, x_sample, ffn1_pre, ffn1_w_gu, ffn1_w_down, ffn1_post, mix_pre, w_in, b_gate, q_norm, kv_norm, w_uq, w_ukv, conv_w, w_conv_out, w_o, mix_post, ffn2_pre, ffn2_w_gu, ffn2_w_down, ffn2_post):
    params = (ffn1_pre, ffn1_w_gu, ffn1_w_down, ffn1_post, mix_pre, w_in, b_gate, q_norm,
              kv_norm, w_uq, w_ukv, conv_w, w_conv_out, w_o, mix_post, ffn2_pre, ffn2_w_gu,
              ffn2_w_down, ffn2_post)
    depth = ffn1_pre.shape[0]
    tables = _rope_tables(max(x_prompt.shape[1], x_sample.shape[1]))
    y_prompt, y_sample = x_prompt, x_sample
    for l in range(depth):
        w = _prep_weights(*(t[l] for t in params))
        y_prompt = _layer(y_prompt, w, tables)
        y_sample = _layer(y_sample, w, tables)
    return (y_prompt, y_sample)
```
